```python
import jax, jax.numpy as jnp
from jax import lax
import numpy as np

D_MODEL = 1024
BATCH = 4
SEQ = 8192
DEPTH = 1

MIX_WIDTH = D_MODEL
CONV_CH = MIX_WIDTH // 2
POOL_WIDTH = MIX_WIDTH - CONV_CH
CONV_K = 3
POOL_WINDOWS = (2, 4, 8, 16)
N_POOL_GROUPS = len(POOL_WINDOWS)
POOL_GROUP_DIM = POOL_WIDTH // N_POOL_GROUPS
IN_PROJ_WIDTH = 3 * CONV_CH + POOL_WIDTH
D_FF = 4 * D_MODEL
N_MOD = 6
LN_EPS = 1e-5
DEEPNORM_ALPHA = (2.0 * DEPTH) ** 0.25
DEEPNORM_BETA = (8.0 * DEPTH) ** -0.25
ADA_INIT = 0.25

kernel_name = "hybrid_conv_pool_sqrelu_deepnorm_adaln"


def _layer_norm(x, g, b):
    xf = x.astype(jnp.float32)
    mu = jnp.mean(xf, axis=-1, keepdims=True)
    var = jnp.mean(jnp.square(xf - mu), axis=-1, keepdims=True)
    y = (xf - mu) * lax.rsqrt(var + LN_EPS) * g.astype(jnp.float32) + b.astype(jnp.float32)
    return y.astype(x.dtype)


def _short_conv(u, w):
    s = u.shape[1]
    up = jnp.pad(u, ((0, 0), (CONV_K - 1, 0), (0, 0)))
    y = up[:, 0:s] * w[0]
    for k in range(1, CONV_K):
        y = y + up[:, k:k + s] * w[k]
    return y


def _multiscale_pool(u, w_pool, pool_scale):
    b, s, _ = u.shape
    uf = u.astype(jnp.float32)
    cs = jnp.cumsum(uf, axis=1)
    pos = jnp.arange(1, s + 1, dtype=jnp.float32)[None, :, None]
    outs = []
    for gi, win in enumerate(POOL_WINDOWS):
        sl = slice(gi * POOL_GROUP_DIM, (gi + 1) * POOL_GROUP_DIM)
        cs_g = cs[..., sl]
        prev = jnp.pad(cs_g, ((0, 0), (win, 0), (0, 0)))[:, :s]
        mean = (cs_g - prev) / jnp.minimum(pos, float(win))
        outs.append(mean - uf[..., sl])
    p = jnp.stack(outs, axis=2)
    p = jnp.einsum("bsgc,gcd->bsgd", p, w_pool.astype(jnp.float32))
    p = p.reshape(b, s, POOL_WIDTH) * pool_scale.astype(jnp.float32)
    return p.astype(u.dtype)


def setup_inputs(seed: int = 0) -> dict:
    key = jax.random.key(seed)
    ks = jax.random.split(key, 16)
    f32 = jnp.float32
    x = jax.random.normal(ks[0], (BATCH, SEQ, D_MODEL), f32)
    c = jax.random.normal(ks[1], (BATCH, D_MODEL), f32)
    w_ada = jax.random.normal(ks[2], (DEPTH, D_MODEL, N_MOD * D_MODEL), f32) * (ADA_INIT * D_MODEL ** -0.5)
    b_ada = 0.02 * jax.random.normal(ks[3], (DEPTH, N_MOD * D_MODEL), f32)
    w_in = jax.random.normal(ks[4], (DEPTH, D_MODEL, IN_PROJ_WIDTH), f32) * D_MODEL ** -0.5
    conv_w = jax.random.normal(ks[5], (DEPTH, CONV_K, CONV_CH), f32) * CONV_K ** -0.5
    w_pool = jax.random.normal(ks[6], (DEPTH, N_POOL_GROUPS, POOL_GROUP_DIM, POOL_GROUP_DIM), f32) * POOL_GROUP_DIM ** -0.5
    pool_scale = 1.0 + 0.1 * jax.random.normal(ks[7], (DEPTH, POOL_WIDTH), f32)
    w_out = jax.random.normal(ks[8], (DEPTH, MIX_WIDTH, D_MODEL), f32) * (DEEPNORM_BETA * MIX_WIDTH ** -0.5)
    ln1_g = 1.0 + 0.02 * jax.random.normal(ks[9], (DEPTH, D_MODEL), f32)
    ln1_b = 0.02 * jax.random.normal(ks[10], (DEPTH, D_MODEL), f32)
    w_mlp_in = jax.random.normal(ks[11], (DEPTH, D_MODEL, D_FF), f32) * D_MODEL ** -0.5
    w_mlp_out = jax.random.normal(ks[12], (DEPTH, D_FF, D_MODEL), f32) * (DEEPNORM_BETA * D_FF ** -0.5)
    ln2_g = 1.0 + 0.02 * jax.random.normal(ks[13], (DEPTH, D_MODEL), f32)
    ln2_b = 0.02 * jax.random.normal(ks[14], (DEPTH, D_MODEL), f32)
    return {"x": x, "c": c, "w_ada": w_ada, "b_ada": b_ada, "w_in": w_in, "conv_w": conv_w,
            "w_pool": w_pool, "pool_scale": pool_scale, "w_out": w_out, "ln1_g": ln1_g,
            "ln1_b": ln1_b, "w_mlp_in": w_mlp_in, "w_mlp_out": w_mlp_out, "ln2_g": ln2_g,
            "ln2_b": ln2_b}


def reference(x, c, w_ada, b_ada, w_in, conv_w, w_pool, pool_scale, w_out, ln1_g, ln1_b,
              w_mlp_in, w_mlp_out, ln2_g, ln2_b):
    cond = jax.nn.silu(c)
    for l in range(DEPTH):
        mod = (cond @ w_ada[l] + b_ada[l])[:, None, :]
        sh1, sc1, g1, sh2, sc2, g2 = jnp.split(mod, N_MOD, axis=-1)

        h = x * (1.0 + sc1) + sh1
        z = h @ w_in[l]
        gate_b, gate_c, v_conv, v_pool = jnp.split(
            z, [CONV_CH, 2 * CONV_CH, 3 * CONV_CH], axis=-1)
        y_conv = gate_b * _short_conv(gate_c * v_conv, conv_w[l])
        y_pool = _multiscale_pool(v_pool, w_pool[l], pool_scale[l])
        mix = jnp.concatenate([y_conv, y_pool], axis=-1) @ w_out[l]
        x = _layer_norm(DEEPNORM_ALPHA * x + (1.0 + g1) * mix, ln1_g[l], ln1_b[l])

        h = x * (1.0 + sc2) + sh2
        f = jnp.square(jax.nn.relu(h @ w_mlp_in[l])) @ w_mlp_out[l]
        x = _layer_norm(DEEPNORM_ALPHA * x + (1.0 + g2) * f, ln2_g[l], ln2_b[l])
    return x
```

```python
import functools

import jax
import jax.numpy as jnp
from jax import lax
from jax.experimental import pallas as pl
from jax.experimental.pallas import tpu as pltpu

CONV_K = 3
POOL_WINDOWS = (2, 4, 8, 16)
N_MOD = 6
LN_EPS = 1e-5

CONV_HALO = 8
POOL_HALO = 16
POOL_GROUPS_PER_TILE = 2

TOKEN_TILE = 512
FF_CHUNK = 1024
MOD_COL_TILE = 1024
VMEM_LIMIT_BYTES = 56 * 1024 * 1024


def _mod_kernel(c_ref, w_ref, b_ref, o_ref):
    c = c_ref[...]
    cond = c * jax.nn.sigmoid(c)
    o_ref[...] = jnp.dot(cond.astype(jnp.bfloat16), w_ref[...].astype(jnp.bfloat16),
                         preferred_element_type=jnp.float32) + b_ref[...]


def _adaln_mod(c, w_ada, b_ada):
    batch, d = c.shape
    n = w_ada.shape[1]
    return pl.pallas_call(
        _mod_kernel,
        grid=(n // MOD_COL_TILE,),
        in_specs=[
            pl.BlockSpec((batch, d), lambda i: (0, 0)),
            pl.BlockSpec((d, MOD_COL_TILE), lambda i: (0, i)),
            pl.BlockSpec((1, MOD_COL_TILE), lambda i: (0, i)),
        ],
        out_specs=pl.BlockSpec((batch, MOD_COL_TILE), lambda i: (0, i)),
        out_shape=jax.ShapeDtypeStruct((batch, n), jnp.float32),
        name="adaln_mod",
    )(c, w_ada, b_ada.reshape(1, n))


def _layer_norm(r, g, b):
    mu = jnp.mean(r, axis=-1, keepdims=True)
    d = r - mu
    var = jnp.mean(d * d, axis=-1, keepdims=True)
    return d * lax.rsqrt(var + LN_EPS) * g + b


def _layer_kernel(x_ref, mod_ref, w_in_ref, conv_w_ref, w_pool_ref, pool_scale_ref, w_out_ref,
                  ln1_g_ref, ln1_b_ref, w1_ref, w2_ref, ln2_g_ref, ln2_b_ref, o_ref,
                  ext_u, ext_v, wp_bd, *, alpha):
    tm = x_ref.shape[1]
    conv_ch = conv_w_ref.shape[1]
    n_groups, group_dim, _ = w_pool_ref.shape
    d_ff = w1_ref.shape[1]
    j = pl.program_id(1)

    @pl.when(j == 0)
    def _():
        ext_u[0:CONV_HALO, :] = jnp.zeros((CONV_HALO, ext_u.shape[1]), ext_u.dtype)
        ext_v[0:POOL_HALO, :] = jnp.zeros((POOL_HALO, ext_v.shape[1]), ext_v.dtype)

    @pl.when((pl.program_id(0) == 0) & (j == 0))
    def _():
        wp_bd[...] = jnp.zeros(wp_bd.shape, wp_bd.dtype)
        for g in range(n_groups):
            t, k = divmod(g, POOL_GROUPS_PER_TILE)
            wp_bd[t, k * group_dim:(k + 1) * group_dim, k * group_dim:(k + 1) * group_dim] = (
                w_pool_ref[g].astype(wp_bd.dtype))

    x = x_ref[0]
    mod = mod_ref[0]
    sh1, sc1, g1, sh2, sc2, g2 = (mod[i:i + 1] for i in range(N_MOD))

    h = (x * (1.0 + sc1) + sh1).astype(jnp.bfloat16)
    z = jnp.dot(h, w_in_ref[...], preferred_element_type=jnp.float32)
    gate_b = z[:, 0:conv_ch]
    gate_c = z[:, conv_ch:2 * conv_ch]
    v_conv = z[:, 2 * conv_ch:3 * conv_ch]
    v_pool = z[:, 3 * conv_ch:]

    u = gate_c * v_conv
    ext_u[CONV_HALO:CONV_HALO + tm, :] = u
    ext_v[POOL_HALO:POOL_HALO + tm, :] = v_pool

    cw = conv_w_ref[...]
    conv = u * cw[CONV_K - 1:CONV_K]
    for k in range(CONV_K - 1):
        shift = CONV_K - 1 - k
        conv = conv + ext_u[pl.ds(CONV_HALO - shift, tm), :] * cw[k:k + 1]
    y_conv = gate_b * conv

    pos = (j * tm + lax.broadcasted_iota(jnp.int32, (tm, 1), 0) + 1).astype(jnp.float32)
    pooled = []
    for g, win in enumerate(POOL_WINDOWS):
        lanes = slice(g * group_dim, (g + 1) * group_dim)
        v_g = v_pool[:, lanes]
        s = v_g
        for k in range(1, win):
            s = s + ext_v[pl.ds(POOL_HALO - k, tm), lanes]
        pooled.append(s / jnp.minimum(pos, float(win)) - v_g)
    p_in = jnp.concatenate(pooled, axis=1).astype(jnp.bfloat16)
    tile_w = POOL_GROUPS_PER_TILE * group_dim
    p = jnp.concatenate(
        [jnp.dot(p_in[:, t * tile_w:(t + 1) * tile_w], wp_bd[t], preferred_element_type=jnp.float32)
         for t in range(n_groups // POOL_GROUPS_PER_TILE)], axis=1)
    y_pool = p * pool_scale_ref[...]

    ext_u[0:CONV_HALO, :] = ext_u[tm:tm + CONV_HALO, :]
    ext_v[0:POOL_HALO, :] = ext_v[tm:tm + POOL_HALO, :]

    y = jnp.concatenate([y_conv, y_pool], axis=1).astype(jnp.bfloat16)
    mix = jnp.dot(y, w_out_ref[...], preferred_element_type=jnp.float32)
    x1 = _layer_norm(alpha * x + (1.0 + g1) * mix, ln1_g_ref[...], ln1_b_ref[...])

    h2 = (x1 * (1.0 + sc2) + sh2).astype(jnp.bfloat16)
    f = jnp.zeros_like(x1)
    for c0 in range(0, d_ff, FF_CHUNK):
        hid = jnp.dot(h2, w1_ref[:, c0:c0 + FF_CHUNK], preferred_element_type=jnp.float32)
        hid = jnp.square(jnp.maximum(hid, 0.0)).astype(jnp.bfloat16)
        f = f + jnp.dot(hid, w2_ref[c0:c0 + FF_CHUNK, :], preferred_element_type=jnp.float32)
    o_ref[0] = _layer_norm(alpha * x1 + (1.0 + g2) * f, ln2_g_ref[...], ln2_b_ref[...])


def _const_spec(shape):
    return pl.BlockSpec(shape, lambda b, j: (0,) * len(shape), pipeline_mode=pl.Buffered(1))


def _layer(x, mod, w_in, conv_w, w_pool, pool_scale, w_out, ln1_g, ln1_b, w1, w2, ln2_g, ln2_b,
           *, alpha):
    batch, seq, d = x.shape
    conv_ch = conv_w.shape[1]
    n_groups, group_dim, _ = w_pool.shape
    pool_width = n_groups * group_dim
    tm = TOKEN_TILE
    assert seq % tm == 0 and tm >= POOL_HALO and max(POOL_WINDOWS) <= POOL_HALO
    assert n_groups % POOL_GROUPS_PER_TILE == 0 and w1.shape[1] % FF_CHUNK == 0
    bf16 = jnp.bfloat16
    row = lambda a: a.reshape(1, -1)
    operands = (x, mod, w_in.astype(bf16), conv_w, w_pool, row(pool_scale), w_out.astype(bf16),
                row(ln1_g), row(ln1_b), w1.astype(bf16), w2.astype(bf16), row(ln2_g), row(ln2_b))
    in_specs = [pl.BlockSpec((1, tm, d), lambda b, j: (b, j, 0)),
                pl.BlockSpec((1, N_MOD, d), lambda b, j: (b, 0, 0))]
    in_specs += [_const_spec(a.shape) for a in operands[2:]]
    return pl.pallas_call(
        functools.partial(_layer_kernel, alpha=alpha),
        grid=(batch, seq // tm),
        in_specs=in_specs,
        out_specs=pl.BlockSpec((1, tm, d), lambda b, j: (b, j, 0)),
        out_shape=jax.ShapeDtypeStruct(x.shape, x.dtype),
        scratch_shapes=[
            pltpu.VMEM((CONV_HALO + tm, conv_ch), jnp.float32),
            pltpu.VMEM((POOL_HALO + tm, pool_width), jnp.float32),
            pltpu.VMEM((n_groups // POOL_GROUPS_PER_TILE,
                        POOL_GROUPS_PER_TILE * group_dim, POOL_GROUPS_PER_TILE * group_dim), bf16),
        ],
        compiler_params=pltpu.CompilerParams(
            dimension_semantics=("arbitrary", "arbitrary"),
            vmem_limit_bytes=VMEM_LIMIT_BYTES),
        name="hybrid_layer",
    )(*operands)


def kernel(x, c, w_ada, b_ada, w_in, conv_w, w_pool, pool_scale, w_out, ln1_g, ln1_b, w_mlp_in,
           w_mlp_out, ln2_g, ln2_b):
    depth = w_in.shape[0]
    d = x.shape[-1]
    alpha = (2.0 * depth) ** 0.25
    for l in range(depth):
        mod = _adaln_mod(c, w_ada[l], b_ada[l]).reshape(c.shape[0], N_MOD, d)
        x = _layer(x, mod, w_in[l], conv_w[l], w_pool[l], pool_scale[l], w_out[l], ln1_g[l],
                   ln1_b[l], w_mlp_in[l], w_mlp_out[l], ln2_g[l], ln2_b[l], alpha=alpha)
    return x
```

```python
import functools

import jax
import jax.numpy as jnp
from jax import lax
from jax.experimental import pallas as pl
from jax.experimental.pallas import tpu as pltpu

CONV_K = 3
POOL_WINDOWS = (2, 4, 8, 16)
N_MOD = 6
LN_EPS = 1e-5

CONV_HALO = 8
POOL_HALO = 16
POOL_GROUPS_PER_TILE = 2

TOKEN_TILE = 1024
SUB_TILE = 512
COL_PIECE = 512
FF_CHUNK = 512
MOD_COL_TILE = 1024
VMEM_LIMIT_BYTES = 60 * 1024 * 1024


def _mod_kernel(c_ref, w_ref, b_ref, o_ref):
    c = c_ref[...]
    cond = c * jax.nn.sigmoid(c)
    o_ref[...] = jnp.dot(cond.astype(jnp.bfloat16), w_ref[...].astype(jnp.bfloat16),
                         preferred_element_type=jnp.float32) + b_ref[...]


def _adaln_mod(c, w_ada, b_ada):
    batch, d = c.shape
    n = w_ada.shape[1]
    return pl.pallas_call(
        _mod_kernel,
        grid=(n // MOD_COL_TILE,),
        in_specs=[
            pl.BlockSpec((batch, d), lambda i: (0, 0)),
            pl.BlockSpec((d, MOD_COL_TILE), lambda i: (0, i)),
            pl.BlockSpec((1, MOD_COL_TILE), lambda i: (0, i)),
        ],
        out_specs=pl.BlockSpec((batch, MOD_COL_TILE), lambda i: (0, i)),
        out_shape=jax.ShapeDtypeStruct((batch, n), jnp.float32),
        name="adaln_mod",
    )(c, w_ada, b_ada.reshape(1, n))


def _layer_norm(r, g, b):
    mu = jnp.mean(r, axis=-1, keepdims=True)
    d = r - mu
    var = jnp.mean(d * d, axis=-1, keepdims=True)
    return d * lax.rsqrt(var + LN_EPS) * g + b


def _dot(a, b):
    return jnp.dot(a, b, preferred_element_type=jnp.float32)


def _layer_kernel(x_ref, mod_ref, w_in_ref, conv_w_ref, w_pool_ref, pool_scale_ref, w_out_ref,
                  ln1_g_ref, ln1_b_ref, w1_ref, w2_ref, ln2_g_ref, ln2_b_ref, o_ref,
                  carry_u, carry_v, wp_bd, *, alpha, sub):
    tm = x_ref.shape[1]
    n_sub = tm // sub
    conv_ch = conv_w_ref.shape[1]
    n_groups, group_dim, _ = w_pool_ref.shape
    d_ff = w1_ref.shape[1]
    j = pl.program_id(1)

    @pl.when(j == 0)
    def _():
        carry_u[...] = jnp.zeros(carry_u.shape, carry_u.dtype)
        carry_v[...] = jnp.zeros(carry_v.shape, carry_v.dtype)

    @pl.when((pl.program_id(0) == 0) & (j == 0))
    def _():
        wp_bd[...] = jnp.zeros(wp_bd.shape, wp_bd.dtype)
        for g in range(n_groups):
            t, k = divmod(g, POOL_GROUPS_PER_TILE)
            wp_bd[t, k * group_dim:(k + 1) * group_dim, k * group_dim:(k + 1) * group_dim] = (
                w_pool_ref[g].astype(wp_bd.dtype))

    mod = mod_ref[0]
    sh1, sc1, g1, sh2, sc2, g2 = (mod[i:i + 1] for i in range(N_MOD))
    cw = conv_w_ref[...]

    def load(s, st):
        st["x"] = x_ref[0, pl.ds(s * sub, sub), :]
        st["h"] = (st["x"] * (1.0 + sc1) + sh1).astype(jnp.bfloat16)
        st["z"] = {}

    def in_proj(s, st, n):
        st["z"][n] = _dot(st["h"], w_in_ref[:, n * COL_PIECE:(n + 1) * COL_PIECE])

    def with_history(s, st, key, cur, carry, halo):
        st[key] = cur[sub - halo:, :]
        if s == n_sub - 1:
            carry[...] = st[key]
        return jnp.concatenate([carry[...] if s == 0 else subs[s - 1][key], cur], axis=0)

    def conv(s, st):
        u = st["z"][1] * st["z"][2]
        ext = with_history(s, st, "u_tail", u, carry_u, CONV_HALO)
        acc = u * cw[CONV_K - 1:CONV_K]
        for k in range(CONV_K - 1):
            shifted = pltpu.roll(ext, CONV_K - 1 - k, axis=0)[CONV_HALO:, :]
            acc = acc + shifted * cw[k:k + 1]
        st["y_conv"] = (st["z"][0] * acc).astype(jnp.bfloat16)

    def pool(s, st):
        v_pool = st["z"][3]
        ext = with_history(s, st, "v_tail", v_pool, carry_v, POOL_HALO)
        sums, part = [], ext
        for g, win in enumerate(POOL_WINDOWS):
            part = part + pltpu.roll(part, win // 2, axis=0)
            sums.append(part[POOL_HALO:, 0:group_dim])
            part = part[:, group_dim:]
        pos = (j * tm + s * sub + 1
               + lax.broadcasted_iota(jnp.int32, (sub, 1), 0)).astype(jnp.float32)
        pooled = [
            sums[g] / jnp.minimum(pos, float(win)) - v_pool[:, g * group_dim:(g + 1) * group_dim]
            for g, win in enumerate(POOL_WINDOWS)]
        st["p_in"] = jnp.concatenate(pooled, axis=1).astype(jnp.bfloat16)

    def pool_map(s, st):
        tile_w = POOL_GROUPS_PER_TILE * group_dim
        p = jnp.concatenate(
            [_dot(st["p_in"][:, t * tile_w:(t + 1) * tile_w], wp_bd[t])
             for t in range(n_groups // POOL_GROUPS_PER_TILE)], axis=1)
        st["y_pool"] = (p * pool_scale_ref[...]).astype(jnp.bfloat16)

    def out_proj(s, st, k):
        if k == 0:
            st["mix"] = _dot(st["y_conv"], w_out_ref[0:conv_ch, :])
        else:
            st["mix"] = st["mix"] + _dot(st["y_pool"], w_out_ref[conv_ch:, :])

    def norm1(s, st):
        x1 = _layer_norm(alpha * st["x"] + (1.0 + g1) * st["mix"], ln1_g_ref[...], ln1_b_ref[...])
        st["x1"] = x1
        st["h2"] = (x1 * (1.0 + sc2) + sh2).astype(jnp.bfloat16)
        st["f"] = None

    def mlp(s, st, c):
        c0 = c * FF_CHUNK
        hid = _dot(st["h2"], w1_ref[:, c0:c0 + FF_CHUNK])
        hid = jnp.square(jnp.maximum(hid, 0.0)).astype(jnp.bfloat16)
        part = _dot(hid, w2_ref[c0:c0 + FF_CHUNK, :])
        st["f"] = part if st["f"] is None else st["f"] + part

    def norm2(s, st):
        o_ref[0, pl.ds(s * sub, sub), :] = _layer_norm(
            alpha * st["x1"] + (1.0 + g2) * st["f"], ln2_g_ref[...], ln2_b_ref[...])

    n_in = w_in_ref.shape[1] // COL_PIECE
    n_ff = d_ff // FF_CHUNK

    assert n_sub == 2 and n_in == 4
    subs = [{} for _ in range(n_sub)]
    a, b = subs
    load(0, a)
    for n in range(n_in):
        in_proj(0, a, n)
    load(1, b)
    in_proj(1, b, 0)
    conv(0, a)
    in_proj(1, b, 1)
    pool(0, a)
    in_proj(1, b, 2)
    pool_map(0, a)
    in_proj(1, b, 3)
    out_proj(0, a, 0)
    conv(1, b)
    out_proj(0, a, 1)
    pool(1, b)
    pool_map(1, b)
    out_proj(1, b, 0)
    norm1(0, a)
    out_proj(1, b, 1)
    mlp(0, a, 0)
    norm1(1, b)
    for c in range(1, n_ff):
        mlp(0, a, c)
    mlp(1, b, 0)
    norm2(0, a)
    for c in range(1, n_ff):
        mlp(1, b, c)
    norm2(1, b)


def _const_spec(shape):
    return pl.BlockSpec(shape, lambda b, j: (0,) * len(shape), pipeline_mode=pl.Buffered(1))


def _layer(x, mod, w_in, conv_w, w_pool, pool_scale, w_out, ln1_g, ln1_b, w1, w2, ln2_g, ln2_b,
           *, alpha):
    batch, seq, d = x.shape
    conv_ch = conv_w.shape[1]
    n_groups, group_dim, _ = w_pool.shape
    pool_width = n_groups * group_dim
    tm, sub = TOKEN_TILE, SUB_TILE
    assert seq % tm == 0 and tm % sub == 0 and sub >= POOL_HALO
    assert max(POOL_WINDOWS) <= POOL_HALO and CONV_K - 1 <= CONV_HALO
    assert all(win == 2 ** (g + 1) for g, win in enumerate(POOL_WINDOWS))
    assert n_groups % POOL_GROUPS_PER_TILE == 0 and w1.shape[1] % FF_CHUNK == 0
    assert conv_ch == COL_PIECE and pool_width == COL_PIECE and w_in.shape[1] % COL_PIECE == 0
    bf16 = jnp.bfloat16
    row = lambda a: a.reshape(1, -1)
    operands = (x, mod, w_in.astype(bf16), conv_w, w_pool, row(pool_scale), w_out.astype(bf16),
                row(ln1_g), row(ln1_b), w1.astype(bf16), w2.astype(bf16), row(ln2_g), row(ln2_b))
    in_specs = [pl.BlockSpec((1, tm, d), lambda b, j: (b, j, 0)),
                pl.BlockSpec((1, N_MOD, d), lambda b, j: (b, 0, 0))]
    in_specs += [_const_spec(a.shape) for a in operands[2:]]
    return pl.pallas_call(
        functools.partial(_layer_kernel, alpha=alpha, sub=sub),
        grid=(batch, seq // tm),
        in_specs=in_specs,
        out_specs=pl.BlockSpec((1, tm, d), lambda b, j: (b, j, 0)),
        out_shape=jax.ShapeDtypeStruct(x.shape, x.dtype),
        scratch_shapes=[
            pltpu.VMEM((CONV_HALO, conv_ch), jnp.float32),
            pltpu.VMEM((POOL_HALO, pool_width), jnp.float32),
            pltpu.VMEM((n_groups // POOL_GROUPS_PER_TILE,
                        POOL_GROUPS_PER_TILE * group_dim, POOL_GROUPS_PER_TILE * group_dim), bf16),
        ],
        compiler_params=pltpu.CompilerParams(
            dimension_semantics=("arbitrary", "arbitrary"),
            vmem_limit_bytes=VMEM_LIMIT_BYTES),
        name="hybrid_layer",
    )(*operands)


def kernel(x, c, w_ada, b_ada, w_in, conv_w, w_pool, pool_scale, w_out, ln1_g, ln1_b, w_mlp_in,
           w_mlp_out, ln2_g, ln2_b):
    depth = w_in.shape[0]
    d = x.shape[-1]
    alpha = (2.0 * depth) ** 0.25
    for l in range(depth):
        mod = _adaln_mod(c, w_ada[l], b_ada[l]).reshape(c.shape[0], N_MOD, d)
        x = _layer(x, mod, w_in[l], conv_w[l], w_pool[l], pool_scale[l], w_out[l], ln1_g[l],
                   ln1_b[l], w_mlp_in[l], w_mlp_out[l], ln2_g[l], ln2_b[l], alpha=alpha)
    return x
```

```python
import functools

import jax
import jax.numpy as jnp
from jax import lax
from jax.experimental import pallas as pl
from jax.experimental.pallas import tpu as pltpu

CONV_K = 3
POOL_WINDOWS = (2, 4, 8, 16)
N_MOD = 6
LN_EPS = 1e-5

CONV_HALO = 8
POOL_HALO = 16
POOL_GROUPS_PER_TILE = 2

TILE = 512
TILES_PER_STEP = 2
COL_PIECE = 512
FF_CHUNK = 512
NORM_BLOCKS = 4
PIN_ROWS = 16
MOD_COL_TILE = 1024
VMEM_LIMIT_BYTES = 60 * 1024 * 1024


def _mod_kernel(c_ref, w_ref, b_ref, o_ref):
    c = c_ref[...]
    cond = c * jax.nn.sigmoid(c)
    o_ref[...] = jnp.dot(cond.astype(jnp.bfloat16), w_ref[...].astype(jnp.bfloat16),
                         preferred_element_type=jnp.float32) + b_ref[...]


def _adaln_mod(c, w_ada, b_ada):
    batch, d = c.shape
    n = w_ada.shape[1]
    return pl.pallas_call(
        _mod_kernel,
        grid=(n // MOD_COL_TILE,),
        in_specs=[
            pl.BlockSpec((batch, d), lambda i: (0, 0)),
            pl.BlockSpec((d, MOD_COL_TILE), lambda i: (0, i)),
            pl.BlockSpec((1, MOD_COL_TILE), lambda i: (0, i)),
        ],
        out_specs=pl.BlockSpec((batch, MOD_COL_TILE), lambda i: (0, i)),
        out_shape=jax.ShapeDtypeStruct((batch, n), jnp.float32),
        name="adaln_mod",
    )(c, w_ada, b_ada.reshape(1, n))


def _layer_norm(r, g, b):
    mu = jnp.mean(r, axis=-1, keepdims=True)
    d = r - mu
    var = jnp.mean(d * d, axis=-1, keepdims=True)
    return d * lax.rsqrt(var + LN_EPS) * g + b


def _dot(a, b):
    return jnp.dot(a, b, preferred_element_type=jnp.float32)


def _pack_rows(w):
    k, n = w.shape
    pairs = w.astype(jnp.bfloat16).reshape(k // 2, 2, n).transpose(0, 2, 1)
    return lax.bitcast_convert_type(pairs, jnp.uint32)


def _rows(w_ref, r0, r1, c0, c1):
    return pltpu.bitcast(w_ref[r0 // 2:r1 // 2, c0:c1], jnp.bfloat16)


def _zero_after(*values):
    acc = None
    for v in values:
        words = pltpu.bitcast(v, jnp.uint32)
        for r in range(0, words.shape[0], 8):
            for c in range(0, words.shape[1], 128):
                piece = words[r:r + 8, c:c + 128]
                acc = piece if acc is None else acc | piece
    return lax.shift_right_logical(lax.shift_right_logical(acc, jnp.uint32(16)), jnp.uint32(16))


def _not_before(lhs, head_f32, zero):
    rows, cols = head_f32.shape
    ready = jnp.concatenate([zero] * (cols // 128), axis=1) == 0
    head = [jnp.where(ready, head_f32[r:r + 8, :], 0.0) for r in range(0, rows, 8)]
    return jnp.concatenate([jnp.concatenate(head, axis=0).astype(lhs.dtype), lhs[rows:, :]], axis=0)


def _layer_kernel(xb_ref, xn_ref, mod_ref, w_in_ref, conv_w_ref, w_pool_ref, pool_scale_ref,
                  w_out_ref, ln1_g_ref, ln1_b_ref, w1_ref, w2_ref, ln2_g_ref, ln2_b_ref, o_ref,
                  carry_u, carry_v, wp_bd, mix_s, xa_s, *, alpha, tiles_per_seq, n_tiles):
    tile = xb_ref.shape[1]
    d = xb_ref.shape[2]
    conv_ch = conv_w_ref.shape[1]
    n_groups, group_dim, _ = w_pool_ref.shape
    n_in = w_in_ref.shape[1] // COL_PIECE
    n_ff = w1_ref.shape[1] // FF_CHUNK
    step = pl.program_id(0)
    tile_a = 2 * (step - 1)
    tile_n = jnp.minimum(2 * step, n_tiles - 1)
    cw = conv_w_ref[...]

    def mod_rows(tile_idx):
        row = mod_ref[pl.ds(tile_idx // tiles_per_seq, 1), :]
        return [row[:, i * d:(i + 1) * d] for i in range(N_MOD)]

    def load(st, x_ref, tile_idx):
        st["mod"] = mod_rows(tile_idx)
        st["pos0"] = (tile_idx % tiles_per_seq) * tile
        sh1, sc1 = st["mod"][0], st["mod"][1]
        st["x"] = x_ref[0]
        h = st["x"] * (1.0 + sc1) + sh1
        st["h_head"] = h[0:PIN_ROWS, :]
        st["h"] = h.astype(jnp.bfloat16)
        st["z"] = {}

    def in_proj(st, n, after=None):
        lhs = st["h"]
        if after is not None:
            lhs = _not_before(lhs, st["h_head"], _zero_after(*after))
        st["z"][n] = _dot(lhs, _rows(w_in_ref, 0, d, n * COL_PIECE, (n + 1) * COL_PIECE))

    def conv(st, history):
        u = st["z"][1] * st["z"][2]
        st["u_tail"] = u[tile - CONV_HALO:, :]
        ext = jnp.concatenate([history, u], axis=0)
        acc = u * cw[CONV_K - 1:CONV_K]
        for k in range(CONV_K - 1):
            shifted = pltpu.roll(ext, CONV_K - 1 - k, axis=0)[CONV_HALO:, :]
            acc = acc + shifted * cw[k:k + 1]
        st["y_conv"] = (st["z"][0] * acc).astype(jnp.bfloat16)

    def pool(st, history):
        v_pool = st["z"][3]
        st["v_tail"] = v_pool[tile - POOL_HALO:, :]
        sums, part = [], jnp.concatenate([history, v_pool], axis=0)
        for g, win in enumerate(POOL_WINDOWS):
            part = part + pltpu.roll(part, win // 2, axis=0)
            sums.append(part[POOL_HALO:, 0:group_dim])
            part = part[:, group_dim:]
        pos = (st["pos0"] + 1 + lax.broadcasted_iota(jnp.int32, (tile, 1), 0)).astype(jnp.float32)
        pooled = [
            sums[g] / jnp.minimum(pos, float(win)) - v_pool[:, g * group_dim:(g + 1) * group_dim]
            for g, win in enumerate(POOL_WINDOWS)]
        st["p_in"] = jnp.concatenate(pooled, axis=1).astype(jnp.bfloat16)

    def pool_map(st):
        tile_w = POOL_GROUPS_PER_TILE * group_dim
        p = jnp.concatenate(
            [_dot(st["p_in"][:, t * tile_w:(t + 1) * tile_w], wp_bd[t])
             for t in range(n_groups // POOL_GROUPS_PER_TILE)], axis=1)
        st["y_pool"] = (p * pool_scale_ref[...]).astype(jnp.bfloat16)

    def out_proj(st, k):
        if k == 0:
            st["mix"] = _dot(st["y_conv"], _rows(w_out_ref, 0, conv_ch, 0, d))
        else:
            st["mix"] = st["mix"] + _dot(st["y_pool"], _rows(w_out_ref, conv_ch, d, 0, d))

    blk = tile // NORM_BLOCKS

    def norm1(st, i):
        _, _, g1, sh2, sc2, _ = st["mod"]
        rows = slice(i * blk, (i + 1) * blk)
        x1 = _layer_norm(alpha * st["x"][rows] + (1.0 + g1) * st["mix"][rows],
                         ln1_g_ref[...], ln1_b_ref[...])
        st.setdefault("x1", []).append(x1)
        st.setdefault("h2", []).append((x1 * (1.0 + sc2) + sh2).astype(jnp.bfloat16))
        st["f"] = None

    def mlp(st, c):
        if isinstance(st["h2"], list):
            st["h2"] = jnp.concatenate(st["h2"], axis=0)
        c0 = c * FF_CHUNK
        hid = _dot(st["h2"], _rows(w1_ref, 0, d, c0, c0 + FF_CHUNK))
        hid = jnp.square(jnp.maximum(hid, 0.0)).astype(jnp.bfloat16)
        part = _dot(hid, _rows(w2_ref, c0, c0 + FF_CHUNK, 0, d))
        st["f"] = part if st["f"] is None else st["f"] + part

    def norm2(st, row0, i):
        g2 = st["mod"][5]
        rows = slice(i * blk, (i + 1) * blk)
        out = _layer_norm(
            alpha * st["x1"][i] + (1.0 + g2) * st["f"][rows], ln2_g_ref[...], ln2_b_ref[...])
        o_ref[0, row0 + i * blk:row0 + (i + 1) * blk, :] = out
        st.setdefault("out", []).append(out)

    def hand_over(st):
        mix_s[...] = st["mix"]
        carry_u[...] = st["u_tail"]
        carry_v[...] = st["v_tail"]

    @pl.when(step == 0)
    def _():
        wp_bd[...] = jnp.zeros(wp_bd.shape, wp_bd.dtype)
        for g in range(n_groups):
            t, k = divmod(g, POOL_GROUPS_PER_TILE)
            wp_bd[t, k * group_dim:(k + 1) * group_dim, k * group_dim:(k + 1) * group_dim] = (
                w_pool_ref[g].astype(wp_bd.dtype))
        n = {}
        load(n, xn_ref, tile_n)
        xa_s[...] = n["x"]
        for i in range(n_in):
            in_proj(n, i)
        conv(n, jnp.zeros((CONV_HALO, conv_ch), jnp.float32))
        pool(n, jnp.zeros((POOL_HALO, n_groups * group_dim), jnp.float32))
        pool_map(n)
        out_proj(n, 0)
        out_proj(n, 1)
        hand_over(n)

    @pl.when(step > 0)
    def _():
        assert n_in == 4 and n_ff == 8
        a, b, n = {}, {}, {}
        a.update(mod=mod_rows(tile_a), x=xa_s[...], mix=mix_s[...])
        assert NORM_BLOCKS == 4
        load(b, xb_ref, tile_a + 1)
        in_proj(b, 0)
        norm1(a, 0)
        in_proj(b, 1, after=[a["h2"][0]])
        norm1(a, 1)
        in_proj(b, 2, after=[a["h2"][1]])
        norm1(a, 2)
        in_proj(b, 3, after=[a["h2"][2]])
        norm1(a, 3)
        mlp(a, 0)
        conv(b, carry_u[...])
        mlp(a, 1)
        pool(b, carry_v[...])
        mlp(a, 2)
        pool_map(b)
        mlp(a, 3)
        out_proj(b, 0)
        out_proj(b, 1)
        mlp(a, 4)
        norm1(b, 0)
        mlp(a, 5)
        norm1(b, 1)
        mlp(a, 6)
        norm1(b, 2)
        mlp(a, 7)
        norm1(b, 3)
        mlp(b, 0)
        norm2(a, 0, 0)
        load(n, xn_ref, tile_n)
        xa_s[...] = n["x"]
        mlp(b, 1)
        norm2(a, 0, 1)
        in_proj(n, 0)
        in_proj(n, 1)
        mlp(b, 2)
        norm2(a, 0, 2)
        in_proj(n, 2)
        in_proj(n, 3)
        mlp(b, 3)
        norm2(a, 0, 3)
        fresh = tile_n % tiles_per_seq == 0
        mlp(b, 4)
        conv(n, jnp.where(fresh, 0.0, b["u_tail"]))
        mlp(b, 5)
        pool(n, jnp.where(fresh, 0.0, b["v_tail"]))
        mlp(b, 6)
        pool_map(n)
        mlp(b, 7)
        out_proj(n, 0)
        norm2(b, tile, 0)
        norm2(b, tile, 1)
        out_proj(n, 1)
        norm2(b, tile, 2)
        norm2(b, tile, 3)
        hand_over(n)


def _layer(x, mod, w_in, conv_w, w_pool, pool_scale, w_out, ln1_g, ln1_b, w1, w2, ln2_g, ln2_b,
           *, alpha):
    batch, seq, d = x.shape
    conv_ch = conv_w.shape[1]
    n_groups, group_dim, _ = w_pool.shape
    pool_width = n_groups * group_dim
    assert seq % (TILE * TILES_PER_STEP) == 0 and TILE >= POOL_HALO and TILES_PER_STEP == 2
    assert max(POOL_WINDOWS) <= POOL_HALO and CONV_K - 1 <= CONV_HALO
    assert all(win == 2 ** (g + 1) for g, win in enumerate(POOL_WINDOWS))
    assert n_groups % POOL_GROUPS_PER_TILE == 0 and w1.shape[1] % FF_CHUNK == 0
    assert conv_ch == COL_PIECE and pool_width == COL_PIECE and w_in.shape[1] % COL_PIECE == 0
    tiles_per_seq = seq // TILE
    n_tiles = batch * tiles_per_seq
    steps_per_seq = tiles_per_seq // TILES_PER_STEP
    bf16 = jnp.bfloat16
    row = lambda a: a.reshape(1, -1)
    consts = (mod, _pack_rows(w_in), conv_w, w_pool, row(pool_scale), _pack_rows(w_out),
              row(ln1_g), row(ln1_b), _pack_rows(w1), _pack_rows(w2), row(ln2_g), row(ln2_b))

    def x_tile(tile_of_step):
        def index_map(s):
            t = tile_of_step(s)
            return (t // tiles_per_seq, t % tiles_per_seq, 0)
        return pl.BlockSpec((1, TILE, d), index_map)

    def const_spec(shape):
        return pl.BlockSpec(shape, lambda s: (0,) * len(shape), pipeline_mode=pl.Buffered(1))

    def out_index(s):
        g = jnp.maximum(s - 1, 0)
        return (g // steps_per_seq, g % steps_per_seq, 0)

    return pl.pallas_call(
        functools.partial(_layer_kernel, alpha=alpha, tiles_per_seq=tiles_per_seq,
                          n_tiles=n_tiles),
        grid=(n_tiles // TILES_PER_STEP + 1,),
        in_specs=[x_tile(lambda s: jnp.maximum(2 * s - 1, 0)),
                  x_tile(lambda s: jnp.minimum(2 * s, n_tiles - 1))]
        + [const_spec(a.shape) for a in consts],
        out_specs=pl.BlockSpec((1, TILES_PER_STEP * TILE, d), out_index),
        out_shape=jax.ShapeDtypeStruct(x.shape, x.dtype),
        scratch_shapes=[
            pltpu.VMEM((CONV_HALO, conv_ch), jnp.float32),
            pltpu.VMEM((POOL_HALO, pool_width), jnp.float32),
            pltpu.VMEM((n_groups // POOL_GROUPS_PER_TILE,
                        POOL_GROUPS_PER_TILE * group_dim, POOL_GROUPS_PER_TILE * group_dim), bf16),
            pltpu.VMEM((TILE, d), jnp.float32),
            pltpu.VMEM((TILE, d), jnp.float32),
        ],
        compiler_params=pltpu.CompilerParams(
            dimension_semantics=("arbitrary",),
            vmem_limit_bytes=VMEM_LIMIT_BYTES),
        name="hybrid_layer",
    )(x, x, *consts)


def kernel(x, c, w_ada, b_ada, w_in, conv_w, w_pool, pool_scale, w_out, ln1_g, ln1_b, w_mlp_in,
           w_mlp_out, ln2_g, ln2_b):
    depth = w_in.shape[0]
    alpha = (2.0 * depth) ** 0.25
    for l in range(depth):
        mod = _adaln_mod(c, w_ada[l], b_ada[l])
        x = _layer(x, mod, w_in[l], conv_w[l], w_pool[l], pool_scale[l], w_out[l], ln1_g[l],
                   ln1_b[l], w_mlp_in[l], w_mlp_out[l], ln2_g[l], ln2_b[l], alpha=alpha)
    return x
```

```python
import functools

import jax
import jax.numpy as jnp
from jax import lax
from jax.experimental import pallas as pl
from jax.experimental.pallas import tpu as pltpu

CONV_K = 3
POOL_WINDOWS = (2, 4, 8, 16)
N_MOD = 6
LN_EPS = 1e-5

CONV_HALO = 8
POOL_HALO = 16
POOL_GROUPS_PER_TILE = 2

TOKEN_TILE = 1024
SUB_TILE = 512
COL_PIECE = 512
FF_CHUNK = 512
MOD_COL_TILE = 1024
VMEM_LIMIT_BYTES = 60 * 1024 * 1024


def _mod_kernel(c_ref, w_ref, b_ref, o_ref):
    c = c_ref[...]
    cond = c * jax.nn.sigmoid(c)
    o_ref[...] = jnp.dot(cond.astype(jnp.bfloat16), w_ref[...].astype(jnp.bfloat16),
                         preferred_element_type=jnp.float32) + b_ref[...]


def _adaln_mod(c, w_ada, b_ada):
    batch, d = c.shape
    n = w_ada.shape[1]
    return pl.pallas_call(
        _mod_kernel,
        grid=(n // MOD_COL_TILE,),
        in_specs=[
            pl.BlockSpec((batch, d), lambda i: (0, 0)),
            pl.BlockSpec((d, MOD_COL_TILE), lambda i: (0, i)),
            pl.BlockSpec((1, MOD_COL_TILE), lambda i: (0, i)),
        ],
        out_specs=pl.BlockSpec((batch, MOD_COL_TILE), lambda i: (0, i)),
        out_shape=jax.ShapeDtypeStruct((batch, n), jnp.float32),
        name="adaln_mod",
    )(c, w_ada, b_ada.reshape(1, n))


def _layer_norm(r, g, b):
    mu = jnp.mean(r, axis=-1, keepdims=True)
    d = r - mu
    var = jnp.mean(d * d, axis=-1, keepdims=True)
    return d * lax.rsqrt(var + LN_EPS) * g + b


def _dot(a, b):
    return jnp.dot(a, b, preferred_element_type=jnp.float32)


def _layer_kernel(x_ref, mod_ref, w_in_ref, conv_w_ref, w_pool_ref, pool_scale_ref, w_out_ref,
                  ln1_g_ref, ln1_b_ref, w1_ref, w2_ref, ln2_g_ref, ln2_b_ref, o_ref,
                  carry_u, carry_v, wp_bd, *, alpha, sub):
    tm = x_ref.shape[1]
    n_sub = tm // sub
    conv_ch = conv_w_ref.shape[1]
    n_groups, group_dim, _ = w_pool_ref.shape
    d_ff = w1_ref.shape[1]
    j = pl.program_id(1)

    @pl.when(j == 0)
    def _():
        carry_u[...] = jnp.zeros(carry_u.shape, carry_u.dtype)
        carry_v[...] = jnp.zeros(carry_v.shape, carry_v.dtype)

    @pl.when((pl.program_id(0) == 0) & (j == 0))
    def _():
        wp_bd[...] = jnp.zeros(wp_bd.shape, wp_bd.dtype)
        for g in range(n_groups):
            t, k = divmod(g, POOL_GROUPS_PER_TILE)
            wp_bd[t, k * group_dim:(k + 1) * group_dim, k * group_dim:(k + 1) * group_dim] = (
                w_pool_ref[g].astype(wp_bd.dtype))

    mod = mod_ref[0]
    sh1, sc1, g1, sh2, sc2, g2 = (mod[i:i + 1] for i in range(N_MOD))
    cw = conv_w_ref[...]

    def load(s, st):
        st["x"] = x_ref[0, pl.ds(s * sub, sub), :]
        st["h"] = (st["x"] * (1.0 + sc1) + sh1).astype(jnp.bfloat16)
        st["z"] = {}

    def in_proj(s, st, n):
        st["z"][n] = _dot(st["h"], w_in_ref[:, n * COL_PIECE:(n + 1) * COL_PIECE])

    def with_history(s, st, key, cur, carry, halo):
        st[key] = cur[sub - halo:, :]
        if s == n_sub - 1:
            carry[...] = st[key]
        return jnp.concatenate([carry[...] if s == 0 else subs[s - 1][key], cur], axis=0)

    def conv(s, st):
        u = st["z"][1] * st["z"][2]
        ext = with_history(s, st, "u_tail", u, carry_u, CONV_HALO)
        acc = u * cw[CONV_K - 1:CONV_K]
        for k in range(CONV_K - 1):
            shifted = pltpu.roll(ext, CONV_K - 1 - k, axis=0)[CONV_HALO:, :]
            acc = acc + shifted * cw[k:k + 1]
        st["y_conv"] = (st["z"][0] * acc).astype(jnp.bfloat16)

    def pool(s, st):
        v_pool = st["z"][3]
        ext = with_history(s, st, "v_tail", v_pool, carry_v, POOL_HALO)
        sums, part = [], ext
        for g, win in enumerate(POOL_WINDOWS):
            part = part + pltpu.roll(part, win // 2, axis=0)
            sums.append(part[POOL_HALO:, 0:group_dim])
            part = part[:, group_dim:]
        pos = (j * tm + s * sub + 1
               + lax.broadcasted_iota(jnp.int32, (sub, 1), 0)).astype(jnp.float32)
        pooled = [
            sums[g] / jnp.minimum(pos, float(win)) - v_pool[:, g * group_dim:(g + 1) * group_dim]
            for g, win in enumerate(POOL_WINDOWS)]
        st["p_in"] = jnp.concatenate(pooled, axis=1).astype(jnp.bfloat16)

    def pool_map(s, st):
        tile_w = POOL_GROUPS_PER_TILE * group_dim
        p = jnp.concatenate(
            [_dot(st["p_in"][:, t * tile_w:(t + 1) * tile_w], wp_bd[t])
             for t in range(n_groups // POOL_GROUPS_PER_TILE)], axis=1)
        st["y_pool"] = (p * pool_scale_ref[...]).astype(jnp.bfloat16)

    def out_proj(s, st, k):
        if k == 0:
            st["mix"] = _dot(st["y_conv"], w_out_ref[0:conv_ch, :])
        else:
            st["mix"] = st["mix"] + _dot(st["y_pool"], w_out_ref[conv_ch:, :])

    def norm1(s, st):
        x1 = _layer_norm(alpha * st["x"] + (1.0 + g1) * st["mix"], ln1_g_ref[...], ln1_b_ref[...])
        st["x1"] = x1
        st["h2"] = (x1 * (1.0 + sc2) + sh2).astype(jnp.bfloat16)

    def mlp_in(s, st, c):
        c0 = c * FF_CHUNK
        hid = _dot(st["h2"], w1_ref[:, c0:c0 + FF_CHUNK])
        st.setdefault("hid", []).append(jnp.square(jnp.maximum(hid, 0.0)).astype(jnp.bfloat16))

    def mlp_out(s, st):
        st["f"] = _dot(jnp.concatenate(st["hid"], axis=1), w2_ref[...])

    def norm2(s, st):
        o_ref[0, pl.ds(s * sub, sub), :] = _layer_norm(
            alpha * st["x1"] + (1.0 + g2) * st["f"], ln2_g_ref[...], ln2_b_ref[...])

    n_in = w_in_ref.shape[1] // COL_PIECE
    n_ff = d_ff // FF_CHUNK

    assert n_sub == 2 and n_in == 4
    subs = [{} for _ in range(n_sub)]
    a, b = subs
    load(0, a)
    for n in range(n_in):
        in_proj(0, a, n)
    load(1, b)
    in_proj(1, b, 0)
    conv(0, a)
    in_proj(1, b, 1)
    pool(0, a)
    in_proj(1, b, 2)
    pool_map(0, a)
    in_proj(1, b, 3)
    out_proj(0, a, 0)
    conv(1, b)
    out_proj(0, a, 1)
    pool(1, b)
    pool_map(1, b)
    out_proj(1, b, 0)
    norm1(0, a)
    out_proj(1, b, 1)
    mlp_in(0, a, 0)
    norm1(1, b)
    for c in range(1, n_ff):
        mlp_in(0, a, c)
    mlp_out(0, a)
    mlp_in(1, b, 0)
    norm2(0, a)
    for c in range(1, n_ff):
        mlp_in(1, b, c)
    mlp_out(1, b)
    norm2(1, b)


def _const_spec(shape):
    return pl.BlockSpec(shape, lambda b, j: (0,) * len(shape), pipeline_mode=pl.Buffered(1))


def _layer(x, mod, w_in, conv_w, w_pool, pool_scale, w_out, ln1_g, ln1_b, w1, w2, ln2_g, ln2_b,
           *, alpha):
    batch, seq, d = x.shape
    conv_ch = conv_w.shape[1]
    n_groups, group_dim, _ = w_pool.shape
    pool_width = n_groups * group_dim
    tm, sub = TOKEN_TILE, SUB_TILE
    assert seq % tm == 0 and tm % sub == 0 and sub >= POOL_HALO
    assert max(POOL_WINDOWS) <= POOL_HALO and CONV_K - 1 <= CONV_HALO
    assert all(win == 2 ** (g + 1) for g, win in enumerate(POOL_WINDOWS))
    assert n_groups % POOL_GROUPS_PER_TILE == 0 and w1.shape[1] % FF_CHUNK == 0
    assert conv_ch == COL_PIECE and pool_width == COL_PIECE and w_in.shape[1] % COL_PIECE == 0
    bf16 = jnp.bfloat16
    row = lambda a: a.reshape(1, -1)
    operands = (x, mod, w_in.astype(bf16), conv_w, w_pool, row(pool_scale), w_out.astype(bf16),
                row(ln1_g), row(ln1_b), w1.astype(bf16), w2.astype(bf16), row(ln2_g), row(ln2_b))
    in_specs = [pl.BlockSpec((1, tm, d), lambda b, j: (b, j, 0)),
                pl.BlockSpec((1, N_MOD, d), lambda b, j: (b, 0, 0))]
    in_specs += [_const_spec(a.shape) for a in operands[2:]]
    return pl.pallas_call(
        functools.partial(_layer_kernel, alpha=alpha, sub=sub),
        grid=(batch, seq // tm),
        in_specs=in_specs,
        out_specs=pl.BlockSpec((1, tm, d), lambda b, j: (b, j, 0)),
        out_shape=jax.ShapeDtypeStruct(x.shape, x.dtype),
        scratch_shapes=[
            pltpu.VMEM((CONV_HALO, conv_ch), jnp.float32),
            pltpu.VMEM((POOL_HALO, pool_width), jnp.float32),
            pltpu.VMEM((n_groups // POOL_GROUPS_PER_TILE,
                        POOL_GROUPS_PER_TILE * group_dim, POOL_GROUPS_PER_TILE * group_dim), bf16),
        ],
        compiler_params=pltpu.CompilerParams(
            dimension_semantics=("arbitrary", "arbitrary"),
            vmem_limit_bytes=VMEM_LIMIT_BYTES),
        name="hybrid_layer",
    )(*operands)


def kernel(x, c, w_ada, b_ada, w_in, conv_w, w_pool, pool_scale, w_out, ln1_g, ln1_b, w_mlp_in,
           w_mlp_out, ln2_g, ln2_b):
    depth = w_in.shape[0]
    d = x.shape[-1]
    alpha = (2.0 * depth) ** 0.25
    for l in range(depth):
        mod = _adaln_mod(c, w_ada[l], b_ada[l]).reshape(c.shape[0], N_MOD, d)
        x = _layer(x, mod, w_in[l], conv_w[l], w_pool[l], pool_scale[l], w_out[l], ln1_g[l],
                   ln1_b[l], w_mlp_in[l], w_mlp_out[l], ln2_g[l], ln2_b[l], alpha=alpha)
    return x
```

```python
import functools

import jax
import jax.numpy as jnp
from jax import lax
from jax.experimental import pallas as pl
from jax.experimental.pallas import tpu as pltpu

CONV_K = 3
POOL_WINDOWS = (2, 4, 8, 16)
N_MOD = 6
LN_EPS = 1e-5

CONV_HALO = 8
POOL_HALO = 16
POOL_GROUPS_PER_TILE = 2

TILE = 512
TILES_PER_STEP = 2
COL_PIECE = 512
FF_CHUNK = 512
NORM_BLOCKS = 4
PIN_ROWS = 16
MOD_COL_TILE = 1024
VMEM_LIMIT_BYTES = 60 * 1024 * 1024


def _mod_kernel(c_ref, w_ref, b_ref, o_ref):
    c = c_ref[...]
    cond = c * jax.nn.sigmoid(c)
    o_ref[...] = jnp.dot(cond.astype(jnp.bfloat16), w_ref[...].astype(jnp.bfloat16),
                         preferred_element_type=jnp.float32) + b_ref[...]


def _adaln_mod(c, w_ada, b_ada):
    batch, d = c.shape
    n = w_ada.shape[1]
    return pl.pallas_call(
        _mod_kernel,
        grid=(n // MOD_COL_TILE,),
        in_specs=[
            pl.BlockSpec((batch, d), lambda i: (0, 0)),
            pl.BlockSpec((d, MOD_COL_TILE), lambda i: (0, i)),
            pl.BlockSpec((1, MOD_COL_TILE), lambda i: (0, i)),
        ],
        out_specs=pl.BlockSpec((batch, MOD_COL_TILE), lambda i: (0, i)),
        out_shape=jax.ShapeDtypeStruct((batch, n), jnp.float32),
        name="adaln_mod",
    )(c, w_ada, b_ada.reshape(1, n))


def _layer_norm(r, g, b):
    mu = jnp.mean(r, axis=-1, keepdims=True)
    d = r - mu
    var = jnp.mean(d * d, axis=-1, keepdims=True)
    return d * lax.rsqrt(var + LN_EPS) * g + b


def _dot(a, b):
    return jnp.dot(a, b, preferred_element_type=jnp.float32)


def _zero_after(*values):
    acc = None
    for v in values:
        words = pltpu.bitcast(v, jnp.uint32)
        for r in range(0, words.shape[0], 8):
            for c in range(0, words.shape[1], 128):
                piece = words[r:r + 8, c:c + 128]
                acc = piece if acc is None else acc | piece
    return lax.shift_right_logical(lax.shift_right_logical(acc, jnp.uint32(16)), jnp.uint32(16))


def _not_before(lhs, head_f32, zero):
    rows, cols = head_f32.shape
    ready = jnp.concatenate([zero] * (cols // 128), axis=1) == 0
    head = [jnp.where(ready, head_f32[r:r + 8, :], 0.0) for r in range(0, rows, 8)]
    return jnp.concatenate([jnp.concatenate(head, axis=0).astype(lhs.dtype), lhs[rows:, :]], axis=0)


def _layer_kernel(xb_ref, xn_ref, mod_ref, w_in_ref, conv_w_ref, w_pool_ref, pool_scale_ref,
                  w_out_ref, ln1_g_ref, ln1_b_ref, w1_ref, w2_ref, ln2_g_ref, ln2_b_ref, o_ref,
                  carry_u, carry_v, wp_bd, mix_s, xa_s, *, alpha, tiles_per_seq, n_tiles):
    tile = xb_ref.shape[1]
    d = xb_ref.shape[2]
    conv_ch = conv_w_ref.shape[1]
    n_groups, group_dim, _ = w_pool_ref.shape
    n_in = w_in_ref.shape[1] // COL_PIECE
    n_ff = w1_ref.shape[1] // FF_CHUNK
    step = pl.program_id(0)
    tile_a = 2 * (step - 1)
    tile_n = jnp.minimum(2 * step, n_tiles - 1)
    cw = conv_w_ref[...]

    def mod_rows(tile_idx):
        row = mod_ref[pl.ds(tile_idx // tiles_per_seq, 1), :]
        return [row[:, i * d:(i + 1) * d] for i in range(N_MOD)]

    def load(st, x_ref, tile_idx):
        st["mod"] = mod_rows(tile_idx)
        st["pos0"] = (tile_idx % tiles_per_seq) * tile
        sh1, sc1 = st["mod"][0], st["mod"][1]
        st["x"] = x_ref[0]
        h = st["x"] * (1.0 + sc1) + sh1
        st["h_head"] = h[0:PIN_ROWS, :]
        st["h"] = h.astype(jnp.bfloat16)
        st["z"] = {}

    def in_proj(st, n, after=None):
        lhs = st["h"]
        if after is not None:
            lhs = _not_before(lhs, st["h_head"], _zero_after(*after))
        st["z"][n] = _dot(lhs, w_in_ref[:, n * COL_PIECE:(n + 1) * COL_PIECE])

    def conv(st, history):
        u = st["z"][1] * st["z"][2]
        st["u_tail"] = u[tile - CONV_HALO:, :]
        ext = jnp.concatenate([history, u], axis=0)
        acc = u * cw[CONV_K - 1:CONV_K]
        for k in range(CONV_K - 1):
            shifted = pltpu.roll(ext, CONV_K - 1 - k, axis=0)[CONV_HALO:, :]
            acc = acc + shifted * cw[k:k + 1]
        st["y_conv"] = (st["z"][0] * acc).astype(jnp.bfloat16)

    def pool(st, history):
        v_pool = st["z"][3]
        st["v_tail"] = v_pool[tile - POOL_HALO:, :]
        sums, part = [], jnp.concatenate([history, v_pool], axis=0)
        for g, win in enumerate(POOL_WINDOWS):
            part = part + pltpu.roll(part, win // 2, axis=0)
            sums.append(part[POOL_HALO:, 0:group_dim])
            part = part[:, group_dim:]
        pos = (st["pos0"] + 1 + lax.broadcasted_iota(jnp.int32, (tile, 1), 0)).astype(jnp.float32)
        pooled = [
            sums[g] / jnp.minimum(pos, float(win)) - v_pool[:, g * group_dim:(g + 1) * group_dim]
            for g, win in enumerate(POOL_WINDOWS)]
        st["p_in"] = jnp.concatenate(pooled, axis=1).astype(jnp.bfloat16)

    def pool_map(st):
        tile_w = POOL_GROUPS_PER_TILE * group_dim
        p = jnp.concatenate(
            [_dot(st["p_in"][:, t * tile_w:(t + 1) * tile_w], wp_bd[t])
             for t in range(n_groups // POOL_GROUPS_PER_TILE)], axis=1)
        st["y_pool"] = (p * pool_scale_ref[...]).astype(jnp.bfloat16)

    def out_proj(st, k):
        if k == 0:
            st["mix"] = _dot(st["y_conv"], w_out_ref[0:conv_ch, :])
        else:
            st["mix"] = st["mix"] + _dot(st["y_pool"], w_out_ref[conv_ch:, :])

    blk = tile // NORM_BLOCKS

    def norm1(st, i):
        _, _, g1, sh2, sc2, _ = st["mod"]
        rows = slice(i * blk, (i + 1) * blk)
        x1 = _layer_norm(alpha * st["x"][rows] + (1.0 + g1) * st["mix"][rows],
                         ln1_g_ref[...], ln1_b_ref[...])
        st.setdefault("x1", []).append(x1)
        st.setdefault("h2", []).append((x1 * (1.0 + sc2) + sh2).astype(jnp.bfloat16))
        st["f"] = None

    def mlp(st, c):
        if isinstance(st["h2"], list):
            st["h2"] = jnp.concatenate(st["h2"], axis=0)
        c0 = c * FF_CHUNK
        hid = _dot(st["h2"], w1_ref[:, c0:c0 + FF_CHUNK])
        hid = jnp.square(jnp.maximum(hid, 0.0)).astype(jnp.bfloat16)
        part = _dot(hid, w2_ref[c0:c0 + FF_CHUNK, :])
        st["f"] = part if st["f"] is None else st["f"] + part

    def norm2(st, row0, i):
        g2 = st["mod"][5]
        rows = slice(i * blk, (i + 1) * blk)
        o_ref[0, row0 + i * blk:row0 + (i + 1) * blk, :] = _layer_norm(
            alpha * st["x1"][i] + (1.0 + g2) * st["f"][rows], ln2_g_ref[...], ln2_b_ref[...])

    def hand_over(st):
        mix_s[...] = st["mix"]
        carry_u[...] = st["u_tail"]
        carry_v[...] = st["v_tail"]

    @pl.when(step == 0)
    def _():
        wp_bd[...] = jnp.zeros(wp_bd.shape, wp_bd.dtype)
        for g in range(n_groups):
            t, k = divmod(g, POOL_GROUPS_PER_TILE)
            wp_bd[t, k * group_dim:(k + 1) * group_dim, k * group_dim:(k + 1) * group_dim] = (
                w_pool_ref[g].astype(wp_bd.dtype))
        n = {}
        load(n, xn_ref, tile_n)
        xa_s[...] = n["x"]
        for i in range(n_in):
            in_proj(n, i)
        conv(n, jnp.zeros((CONV_HALO, conv_ch), jnp.float32))
        pool(n, jnp.zeros((POOL_HALO, n_groups * group_dim), jnp.float32))
        pool_map(n)
        out_proj(n, 0)
        out_proj(n, 1)
        hand_over(n)

    @pl.when(step > 0)
    def _():
        assert n_in == 4 and n_ff == 8 and NORM_BLOCKS == 4
        a, b, n = {}, {}, {}
        a.update(mod=mod_rows(tile_a), x=xa_s[...], mix=mix_s[...])
        load(b, xb_ref, tile_a + 1)
        in_proj(b, 0)
        norm1(a, 0)
        in_proj(b, 1, after=[a["h2"][0]])
        norm1(a, 1)
        in_proj(b, 2, after=[a["h2"][1]])
        norm1(a, 2)
        in_proj(b, 3, after=[a["h2"][2]])
        norm1(a, 3)
        mlp(a, 0)
        conv(b, carry_u[...])
        mlp(a, 1)
        pool(b, carry_v[...])
        mlp(a, 2)
        pool_map(b)
        mlp(a, 3)
        out_proj(b, 0)
        out_proj(b, 1)
        mlp(a, 4)
        norm1(b, 0)
        mlp(a, 5)
        norm1(b, 1)
        mlp(a, 6)
        norm1(b, 2)
        mlp(a, 7)
        norm1(b, 3)
        mlp(b, 0)
        norm2(a, 0, 0)
        load(n, xn_ref, tile_n)
        xa_s[...] = n["x"]
        mlp(b, 1)
        norm2(a, 0, 1)
        in_proj(n, 0)
        in_proj(n, 1)
        mlp(b, 2)
        norm2(a, 0, 2)
        in_proj(n, 2)
        in_proj(n, 3)
        mlp(b, 3)
        norm2(a, 0, 3)
        fresh = tile_n % tiles_per_seq == 0
        mlp(b, 4)
        conv(n, jnp.where(fresh, 0.0, b["u_tail"]))
        mlp(b, 5)
        pool(n, jnp.where(fresh, 0.0, b["v_tail"]))
        mlp(b, 6)
        pool_map(n)
        mlp(b, 7)
        out_proj(n, 0)
        norm2(b, tile, 0)
        norm2(b, tile, 1)
        out_proj(n, 1)
        norm2(b, tile, 2)
        norm2(b, tile, 3)
        hand_over(n)


def _layer(x, mod, w_in, conv_w, w_pool, pool_scale, w_out, ln1_g, ln1_b, w1, w2, ln2_g, ln2_b,
           *, alpha):
    batch, seq, d = x.shape
    conv_ch = conv_w.shape[1]
    n_groups, group_dim, _ = w_pool.shape
    pool_width = n_groups * group_dim
    assert seq % (TILE * TILES_PER_STEP) == 0 and TILE >= POOL_HALO and TILES_PER_STEP == 2
    assert max(POOL_WINDOWS) <= POOL_HALO and CONV_K - 1 <= CONV_HALO
    assert all(win == 2 ** (g + 1) for g, win in enumerate(POOL_WINDOWS))
    assert n_groups % POOL_GROUPS_PER_TILE == 0 and w1.shape[1] % FF_CHUNK == 0
    assert conv_ch == COL_PIECE and pool_width == COL_PIECE and w_in.shape[1] % COL_PIECE == 0
    tiles_per_seq = seq // TILE
    n_tiles = batch * tiles_per_seq
    steps_per_seq = tiles_per_seq // TILES_PER_STEP
    bf16 = jnp.bfloat16
    row = lambda a: a.reshape(1, -1)
    consts = (mod, w_in.astype(bf16), conv_w, w_pool, row(pool_scale), w_out.astype(bf16),
              row(ln1_g), row(ln1_b), w1.astype(bf16), w2.astype(bf16), row(ln2_g), row(ln2_b))

    def x_tile(tile_of_step):
        def index_map(s):
            t = tile_of_step(s)
            return (t // tiles_per_seq, t % tiles_per_seq, 0)
        return pl.BlockSpec((1, TILE, d), index_map)

    def const_spec(shape):
        return pl.BlockSpec(shape, lambda s: (0,) * len(shape), pipeline_mode=pl.Buffered(1))

    def out_index(s):
        g = jnp.maximum(s - 1, 0)
        return (g // steps_per_seq, g % steps_per_seq, 0)

    return pl.pallas_call(
        functools.partial(_layer_kernel, alpha=alpha, tiles_per_seq=tiles_per_seq,
                          n_tiles=n_tiles),
        grid=(n_tiles // TILES_PER_STEP + 1,),
        in_specs=[x_tile(lambda s: jnp.maximum(2 * s - 1, 0)),
                  x_tile(lambda s: jnp.minimum(2 * s, n_tiles - 1))]
        + [const_spec(a.shape) for a in consts],
        out_specs=pl.BlockSpec((1, TILES_PER_STEP * TILE, d), out_index),
        out_shape=jax.ShapeDtypeStruct(x.shape, x.dtype),
        scratch_shapes=[
            pltpu.VMEM((CONV_HALO, conv_ch), jnp.float32),
            pltpu.VMEM((POOL_HALO, pool_width), jnp.float32),
            pltpu.VMEM((n_groups // POOL_GROUPS_PER_TILE,
                        POOL_GROUPS_PER_TILE * group_dim, POOL_GROUPS_PER_TILE * group_dim), bf16),
            pltpu.VMEM((TILE, d), jnp.float32),
            pltpu.VMEM((TILE, d), jnp.float32),
        ],
        compiler_params=pltpu.CompilerParams(
            dimension_semantics=("arbitrary",),
            vmem_limit_bytes=VMEM_LIMIT_BYTES),
        name="hybrid_layer",
    )(x, x, *consts)


def kernel(x, c, w_ada, b_ada, w_in, conv_w, w_pool, pool_scale, w_out, ln1_g, ln1_b, w_mlp_in,
           w_mlp_out, ln2_g, ln2_b):
    depth = w_in.shape[0]
    alpha = (2.0 * depth) ** 0.25
    for l in range(depth):
        mod = _adaln_mod(c, w_ada[l], b_ada[l])
        x = _layer(x, mod, w_in[l], conv_w[l], w_pool[l], pool_scale[l], w_out[l], ln1_g[l],
                   ln1_b[l], w_mlp_in[l], w_mlp_out[l], ln2_g[l], ln2_b[l], alpha=alpha)
    return x
```

```python
import functools

import jax
import jax.numpy as jnp
from jax import lax
from jax.experimental import pallas as pl
from jax.experimental.pallas import tpu as pltpu

CONV_K = 3
POOL_WINDOWS = (2, 4, 8, 16)
N_MOD = 6
LN_EPS = 1e-5

CONV_HALO = 8
POOL_HALO = 16
POOL_GROUPS_PER_TILE = 2

TOKEN_TILE = 1024
SUB_TILE = 512
COL_PIECE = 512
FF_CHUNK = 512
STAGE_SLOTS = 3
STAGE_ROWS = 512
STAGE_COLS = 1024
MOD_COL_TILE = 1024
VMEM_LIMIT_BYTES = 60 * 1024 * 1024


def _mod_kernel(c_ref, w_ref, b_ref, o_ref):
    c = c_ref[...]
    cond = c * jax.nn.sigmoid(c)
    o_ref[...] = jnp.dot(cond.astype(jnp.bfloat16), w_ref[...].astype(jnp.bfloat16),
                         preferred_element_type=jnp.float32) + b_ref[...]


def _adaln_mod(c, w_ada, b_ada):
    batch, d = c.shape
    n = w_ada.shape[1]
    return pl.pallas_call(
        _mod_kernel,
        grid=(n // MOD_COL_TILE,),
        in_specs=[
            pl.BlockSpec((batch, d), lambda i: (0, 0)),
            pl.BlockSpec((d, MOD_COL_TILE), lambda i: (0, i)),
            pl.BlockSpec((1, MOD_COL_TILE), lambda i: (0, i)),
        ],
        out_specs=pl.BlockSpec((batch, MOD_COL_TILE), lambda i: (0, i)),
        out_shape=jax.ShapeDtypeStruct((batch, n), jnp.float32),
        name="adaln_mod",
    )(c, w_ada, b_ada.reshape(1, n))


def _layer_norm(r, g, b):
    mu = jnp.mean(r, axis=-1, keepdims=True)
    d = r - mu
    var = jnp.mean(d * d, axis=-1, keepdims=True)
    return d * lax.rsqrt(var + LN_EPS) * g + b


def _dot(a, b):
    return jnp.dot(a, b, preferred_element_type=jnp.float32)


def _layer_kernel(x_ref, mod_ref, w_in_hbm, conv_w_ref, w_pool_ref, pool_scale_ref, w_out_hbm,
                  ln1_g_ref, ln1_b_ref, w1_hbm, w2_hbm, ln2_g_ref, ln2_b_ref, o_ref,
                  carry_u, carry_v, wp_bd, w_in_ref, w_out_ref, w1_ref, w2_ref, stage, stage_sem,
                  *, alpha, sub, layer):
    tm = x_ref.shape[1]
    n_sub = tm // sub
    conv_ch = conv_w_ref.shape[1]
    n_groups, group_dim, _ = w_pool_ref.shape
    d_ff = w1_ref.shape[1]
    j = pl.program_id(1)

    @pl.when(j == 0)
    def _():
        carry_u[...] = jnp.zeros(carry_u.shape, carry_u.dtype)
        carry_v[...] = jnp.zeros(carry_v.shape, carry_v.dtype)

    @pl.when((pl.program_id(0) == 0) & (j == 0))
    def _():
        wp_bd[...] = jnp.zeros(wp_bd.shape, wp_bd.dtype)
        for g in range(n_groups):
            t, k = divmod(g, POOL_GROUPS_PER_TILE)
            wp_bd[t, k * group_dim:(k + 1) * group_dim, k * group_dim:(k + 1) * group_dim] = (
                w_pool_ref[g].astype(wp_bd.dtype))

        blocks = [(hbm, dst, r0, c0)
                  for hbm, dst in ((w_in_hbm, w_in_ref), (w_out_hbm, w_out_ref),
                                   (w1_hbm, w1_ref), (w2_hbm, w2_ref))
                  for r0 in range(0, dst.shape[0], STAGE_ROWS)
                  for c0 in range(0, dst.shape[1], STAGE_COLS)]

        def fetch(i):
            hbm, _, r0, c0 = blocks[i]
            return pltpu.make_async_copy(
                hbm.at[layer, pl.ds(r0, STAGE_ROWS), pl.ds(c0, STAGE_COLS)],
                stage.at[i % STAGE_SLOTS], stage_sem.at[i % STAGE_SLOTS])

        ahead = STAGE_SLOTS - 1
        for i in range(min(ahead, len(blocks))):
            fetch(i).start()
        for i, (_, dst, r0, c0) in enumerate(blocks):
            if i + ahead < len(blocks):
                fetch(i + ahead).start()
            fetch(i).wait()
            dst[r0:r0 + STAGE_ROWS, c0:c0 + STAGE_COLS] = (
                stage[i % STAGE_SLOTS].astype(dst.dtype))

    mod = mod_ref[0]
    sh1, sc1, g1, sh2, sc2, g2 = (mod[i:i + 1] for i in range(N_MOD))
    cw = conv_w_ref[...]

    def load(s, st):
        st["x"] = x_ref[0, pl.ds(s * sub, sub), :]
        st["h"] = (st["x"] * (1.0 + sc1) + sh1).astype(jnp.bfloat16)
        st["z"] = {}

    def in_proj(s, st, n):
        st["z"][n] = _dot(st["h"], w_in_ref[:, n * COL_PIECE:(n + 1) * COL_PIECE])

    def with_history(s, st, key, cur, carry, halo):
        st[key] = cur[sub - halo:, :]
        if s == n_sub - 1:
            carry[...] = st[key]
        return jnp.concatenate([carry[...] if s == 0 else subs[s - 1][key], cur], axis=0)

    def conv(s, st):
        u = st["z"][1] * st["z"][2]
        ext = with_history(s, st, "u_tail", u, carry_u, CONV_HALO)
        acc = u * cw[CONV_K - 1:CONV_K]
        for k in range(CONV_K - 1):
            shifted = pltpu.roll(ext, CONV_K - 1 - k, axis=0)[CONV_HALO:, :]
            acc = acc + shifted * cw[k:k + 1]
        st["y_conv"] = (st["z"][0] * acc).astype(jnp.bfloat16)

    def pool(s, st):
        v_pool = st["z"][3]
        ext = with_history(s, st, "v_tail", v_pool, carry_v, POOL_HALO)
        sums, part = [], ext
        for g, win in enumerate(POOL_WINDOWS):
            part = part + pltpu.roll(part, win // 2, axis=0)
            sums.append(part[POOL_HALO:, 0:group_dim])
            part = part[:, group_dim:]
        pos = (j * tm + s * sub + 1
               + lax.broadcasted_iota(jnp.int32, (sub, 1), 0)).astype(jnp.float32)
        pooled = [
            sums[g] / jnp.minimum(pos, float(win)) - v_pool[:, g * group_dim:(g + 1) * group_dim]
            for g, win in enumerate(POOL_WINDOWS)]
        st["p_in"] = jnp.concatenate(pooled, axis=1).astype(jnp.bfloat16)

    def pool_map(s, st):
        tile_w = POOL_GROUPS_PER_TILE * group_dim
        p = jnp.concatenate(
            [_dot(st["p_in"][:, t * tile_w:(t + 1) * tile_w], wp_bd[t])
             for t in range(n_groups // POOL_GROUPS_PER_TILE)], axis=1)
        st["y_pool"] = (p * pool_scale_ref[...]).astype(jnp.bfloat16)

    def out_proj(s, st, k):
        if k == 0:
            st["mix"] = _dot(st["y_conv"], w_out_ref[0:conv_ch, :])
        else:
            st["mix"] = st["mix"] + _dot(st["y_pool"], w_out_ref[conv_ch:, :])

    def norm1(s, st):
        x1 = _layer_norm(alpha * st["x"] + (1.0 + g1) * st["mix"], ln1_g_ref[...], ln1_b_ref[...])
        st["x1"] = x1
        st["h2"] = (x1 * (1.0 + sc2) + sh2).astype(jnp.bfloat16)

    def mlp_in(s, st, c):
        c0 = c * FF_CHUNK
        hid = _dot(st["h2"], w1_ref[:, c0:c0 + FF_CHUNK])
        st.setdefault("hid", []).append(jnp.square(jnp.maximum(hid, 0.0)).astype(jnp.bfloat16))

    def mlp_out(s, st):
        st["f"] = _dot(jnp.concatenate(st["hid"], axis=1), w2_ref[...])

    def norm2(s, st):
        o_ref[0, pl.ds(s * sub, sub), :] = _layer_norm(
            alpha * st["x1"] + (1.0 + g2) * st["f"], ln2_g_ref[...], ln2_b_ref[...])

    n_in = w_in_ref.shape[1] // COL_PIECE
    n_ff = d_ff // FF_CHUNK

    assert n_sub == 2 and n_in == 4
    subs = [{} for _ in range(n_sub)]
    a, b = subs
    load(0, a)
    for n in range(n_in):
        in_proj(0, a, n)
    load(1, b)
    in_proj(1, b, 0)
    conv(0, a)
    in_proj(1, b, 1)
    pool(0, a)
    in_proj(1, b, 2)
    pool_map(0, a)
    in_proj(1, b, 3)
    out_proj(0, a, 0)
    conv(1, b)
    out_proj(0, a, 1)
    pool(1, b)
    pool_map(1, b)
    out_proj(1, b, 0)
    norm1(0, a)
    out_proj(1, b, 1)
    mlp_in(0, a, 0)
    norm1(1, b)
    for c in range(1, n_ff):
        mlp_in(0, a, c)
    mlp_out(0, a)
    mlp_in(1, b, 0)
    norm2(0, a)
    for c in range(1, n_ff):
        mlp_in(1, b, c)
    mlp_out(1, b)
    norm2(1, b)


def _const_spec(shape):
    return pl.BlockSpec(shape, lambda b, j: (0,) * len(shape), pipeline_mode=pl.Buffered(1))


def _layer(x, mod, w_in, conv_w, w_pool, pool_scale, w_out, ln1_g, ln1_b, w1, w2, ln2_g, ln2_b,
           *, alpha, layer):
    batch, seq, d = x.shape
    conv_ch = conv_w.shape[1]
    n_groups, group_dim, _ = w_pool.shape
    pool_width = n_groups * group_dim
    tm, sub = TOKEN_TILE, SUB_TILE
    assert seq % tm == 0 and tm % sub == 0 and sub >= POOL_HALO
    assert max(POOL_WINDOWS) <= POOL_HALO and CONV_K - 1 <= CONV_HALO
    assert all(win == 2 ** (g + 1) for g, win in enumerate(POOL_WINDOWS))
    assert n_groups % POOL_GROUPS_PER_TILE == 0 and w1.shape[2] % FF_CHUNK == 0
    assert conv_ch == COL_PIECE and pool_width == COL_PIECE and w_in.shape[2] % COL_PIECE == 0
    weights = (w_in, w_out, w1, w2)
    assert all(w.shape[1] % STAGE_ROWS == 0 and w.shape[2] % STAGE_COLS == 0 for w in weights)
    bf16 = jnp.bfloat16
    row = lambda a: a.reshape(1, -1)
    operands = (x, mod, w_in, conv_w, w_pool, row(pool_scale), w_out,
                row(ln1_g), row(ln1_b), w1, w2, row(ln2_g), row(ln2_b))
    in_specs = [pl.BlockSpec((1, tm, d), lambda b, j: (b, j, 0)),
                pl.BlockSpec((1, N_MOD, d), lambda b, j: (b, 0, 0))]
    in_specs += [pl.BlockSpec(memory_space=pl.ANY) if any(a is w for w in weights)
                 else _const_spec(a.shape) for a in operands[2:]]
    return pl.pallas_call(
        functools.partial(_layer_kernel, alpha=alpha, sub=sub, layer=layer),
        grid=(batch, seq // tm),
        in_specs=in_specs,
        out_specs=pl.BlockSpec((1, tm, d), lambda b, j: (b, j, 0)),
        out_shape=jax.ShapeDtypeStruct(x.shape, x.dtype),
        scratch_shapes=[
            pltpu.VMEM((CONV_HALO, conv_ch), jnp.float32),
            pltpu.VMEM((POOL_HALO, pool_width), jnp.float32),
            pltpu.VMEM((n_groups // POOL_GROUPS_PER_TILE,
                        POOL_GROUPS_PER_TILE * group_dim, POOL_GROUPS_PER_TILE * group_dim), bf16),
        ] + [pltpu.VMEM(w.shape[1:], bf16) for w in weights] + [
            pltpu.VMEM((STAGE_SLOTS, STAGE_ROWS, STAGE_COLS), jnp.float32),
            pltpu.SemaphoreType.DMA((STAGE_SLOTS,)),
        ],
        compiler_params=pltpu.CompilerParams(
            dimension_semantics=("arbitrary", "arbitrary"),
            vmem_limit_bytes=VMEM_LIMIT_BYTES),
        name="hybrid_layer",
    )(*operands)


def kernel(x, c, w_ada, b_ada, w_in, conv_w, w_pool, pool_scale, w_out, ln1_g, ln1_b, w_mlp_in,
           w_mlp_out, ln2_g, ln2_b):
    depth = w_in.shape[0]
    d = x.shape[-1]
    alpha = (2.0 * depth) ** 0.25
    for l in range(depth):
        mod = _adaln_mod(c, w_ada[l], b_ada[l]).reshape(c.shape[0], N_MOD, d)
        x = _layer(x, mod, w_in, conv_w[l], w_pool[l], pool_scale[l], w_out, ln1_g[l],
                   ln1_b[l], w_mlp_in, w_mlp_out, ln2_g[l], ln2_b[l], alpha=alpha, layer=l)
    return x
```

```python
import functools

import jax
import jax.numpy as jnp
from jax import lax
from jax.experimental import pallas as pl
from jax.experimental.pallas import tpu as pltpu

CONV_K = 3
POOL_WINDOWS = (2, 4, 8, 16)
N_MOD = 6
LN_EPS = 1e-5

CONV_HALO = 8
POOL_HALO = 16
POOL_GROUPS_PER_TILE = 2
MOD_ROWS = 8

TOKEN_TILE = 1024
SUB_TILE = 512
COL_PIECE = 512
FF_CHUNK = 512
LAST_OUT_PARTS = 2
STAGE_SLOTS = 3
STAGE_ROWS = 512
STAGE_COLS = 1024
VMEM_LIMIT_BYTES = 60 * 1024 * 1024


def _layer_norm(r, g, b):
    mu = jnp.mean(r, axis=-1, keepdims=True)
    d = r - mu
    var = jnp.mean(d * d, axis=-1, keepdims=True)
    return d * lax.rsqrt(var + LN_EPS) * g + b


def _dot(a, b):
    return jnp.dot(a, b, preferred_element_type=jnp.float32)


def _layer_kernel(x_ref, c_ref, w_ada_hbm, b_ada_ref, w_in_hbm, conv_w_ref, w_pool_ref,
                  pool_scale_ref, w_out_hbm, ln1_g_ref, ln1_b_ref, w1_hbm, w2_hbm, ln2_g_ref,
                  ln2_b_ref, o_ref,
                  carry_u, carry_v, wp_bd, mod_s, w_in_ref, w_out_ref, w1_ref, w2_ref, stage,
                  stage_sem, *, alpha, sub, layer):
    tm = x_ref.shape[1]
    d = x_ref.shape[2]
    n_sub = tm // sub
    conv_ch = conv_w_ref.shape[1]
    n_groups, group_dim, _ = w_pool_ref.shape
    d_ff = w1_ref.shape[1]
    j = pl.program_id(1)

    @pl.when(j == 0)
    def _():
        carry_u[...] = jnp.zeros(carry_u.shape, carry_u.dtype)
        carry_v[...] = jnp.zeros(carry_v.shape, carry_v.dtype)

    @pl.when((pl.program_id(0) == 0) & (j == 0))
    def _():
        wp_bd[...] = jnp.zeros(wp_bd.shape, wp_bd.dtype)
        for g in range(n_groups):
            t, k = divmod(g, POOL_GROUPS_PER_TILE)
            wp_bd[t, k * group_dim:(k + 1) * group_dim, k * group_dim:(k + 1) * group_dim] = (
                w_pool_ref[g].astype(wp_bd.dtype))

        c = c_ref[...]
        cond = c * jax.nn.sigmoid(c)
        cond = jnp.concatenate(
            [cond, jnp.zeros((MOD_ROWS - c.shape[0], d), cond.dtype)], axis=0).astype(jnp.bfloat16)

        def into_mod(r0, c0, block):
            part = _dot(cond[:, r0:r0 + STAGE_ROWS], block.astype(jnp.bfloat16))
            base = b_ada_ref[:, c0:c0 + STAGE_COLS] if r0 == 0 else mod_s[:, c0:c0 + STAGE_COLS]
            mod_s[:, c0:c0 + STAGE_COLS] = base + part

        def into(dst):
            def keep(r0, c0, block):
                dst[r0:r0 + STAGE_ROWS, c0:c0 + STAGE_COLS] = block.astype(dst.dtype)
            return keep

        blocks = [(hbm, r0, c0, consume)
                  for hbm, shape, consume in (
                      (w_ada_hbm, (d, mod_s.shape[1]), into_mod),
                      (w_in_hbm, w_in_ref.shape, into(w_in_ref)),
                      (w_out_hbm, w_out_ref.shape, into(w_out_ref)),
                      (w1_hbm, w1_ref.shape, into(w1_ref)),
                      (w2_hbm, w2_ref.shape, into(w2_ref)))
                  for r0 in range(0, shape[0], STAGE_ROWS)
                  for c0 in range(0, shape[1], STAGE_COLS)]

        def fetch(i):
            hbm, r0, c0, _ = blocks[i]
            return pltpu.make_async_copy(
                hbm.at[layer, pl.ds(r0, STAGE_ROWS), pl.ds(c0, STAGE_COLS)],
                stage.at[i % STAGE_SLOTS], stage_sem.at[i % STAGE_SLOTS])

        ahead = STAGE_SLOTS - 1
        for i in range(min(ahead, len(blocks))):
            fetch(i).start()
        for i, (_, r0, c0, consume) in enumerate(blocks):
            if i + ahead < len(blocks):
                fetch(i + ahead).start()
            fetch(i).wait()
            consume(r0, c0, stage[i % STAGE_SLOTS])

    mod = mod_s[pl.ds(pl.program_id(0), 1), :]
    sh1, sc1, g1, sh2, sc2, g2 = (mod[:, i * d:(i + 1) * d] for i in range(N_MOD))
    cw = conv_w_ref[...]

    def load(s, st):
        st["x"] = x_ref[0, pl.ds(s * sub, sub), :]
        st["h"] = (st["x"] * (1.0 + sc1) + sh1).astype(jnp.bfloat16)
        st["z"] = {}

    def in_proj(s, st, n):
        st["z"][n] = _dot(st["h"], w_in_ref[:, n * COL_PIECE:(n + 1) * COL_PIECE])

    def with_history(s, st, key, cur, carry, halo):
        st[key] = cur[sub - halo:, :]
        if s == n_sub - 1:
            carry[...] = st[key]
        return jnp.concatenate([carry[...] if s == 0 else subs[s - 1][key], cur], axis=0)

    def conv(s, st):
        u = st["z"][1] * st["z"][2]
        ext = with_history(s, st, "u_tail", u, carry_u, CONV_HALO)
        acc = u * cw[CONV_K - 1:CONV_K]
        for k in range(CONV_K - 1):
            shifted = pltpu.roll(ext, CONV_K - 1 - k, axis=0)[CONV_HALO:, :]
            acc = acc + shifted * cw[k:k + 1]
        st["y_conv"] = (st["z"][0] * acc).astype(jnp.bfloat16)

    def pool(s, st):
        v_pool = st["z"][3]
        ext = with_history(s, st, "v_tail", v_pool, carry_v, POOL_HALO)
        sums, part = [], ext
        for g, win in enumerate(POOL_WINDOWS):
            part = part + pltpu.roll(part, win // 2, axis=0)
            sums.append(part[POOL_HALO:, 0:group_dim])
            part = part[:, group_dim:]
        pos = (j * tm + s * sub + 1
               + lax.broadcasted_iota(jnp.int32, (sub, 1), 0)).astype(jnp.float32)
        pooled = [
            sums[g] / jnp.minimum(pos, float(win)) - v_pool[:, g * group_dim:(g + 1) * group_dim]
            for g, win in enumerate(POOL_WINDOWS)]
        st["p_in"] = jnp.concatenate(pooled, axis=1).astype(jnp.bfloat16)

    def pool_map(s, st):
        tile_w = POOL_GROUPS_PER_TILE * group_dim
        p = jnp.concatenate(
            [_dot(st["p_in"][:, t * tile_w:(t + 1) * tile_w], wp_bd[t])
             for t in range(n_groups // POOL_GROUPS_PER_TILE)], axis=1)
        st["y_pool"] = (p * pool_scale_ref[...]).astype(jnp.bfloat16)

    def out_proj(s, st, k):
        if k == 0:
            st["mix"] = _dot(st["y_conv"], w_out_ref[0:conv_ch, :])
        else:
            st["mix"] = st["mix"] + _dot(st["y_pool"], w_out_ref[conv_ch:, :])

    def norm1(s, st):
        x1 = _layer_norm(alpha * st["x"] + (1.0 + g1) * st["mix"], ln1_g_ref[...], ln1_b_ref[...])
        st["x1"] = x1
        st["h2"] = (x1 * (1.0 + sc2) + sh2).astype(jnp.bfloat16)

    def mlp_in(s, st, c):
        c0 = c * FF_CHUNK
        hid = _dot(st["h2"], w1_ref[:, c0:c0 + FF_CHUNK])
        st.setdefault("hid", []).append(jnp.square(jnp.maximum(hid, 0.0)).astype(jnp.bfloat16))

    def mlp_out(s, st, parts=1):
        hid = jnp.concatenate(st["hid"], axis=1)
        rows = sub // parts
        st["f"] = [_dot(hid[r:r + rows, :], w2_ref[...]) for r in range(0, sub, rows)]

    def norm2(s, st):
        rows = sub // len(st["f"])
        for i, f in enumerate(st["f"]):
            r = slice(i * rows, (i + 1) * rows)
            o_ref[0, pl.ds(s * sub + i * rows, rows), :] = _layer_norm(
                alpha * st["x1"][r] + (1.0 + g2) * f, ln2_g_ref[...], ln2_b_ref[...])

    n_in = w_in_ref.shape[1] // COL_PIECE
    n_ff = d_ff // FF_CHUNK

    assert n_sub == 2 and n_in == 4
    subs = [{} for _ in range(n_sub)]
    a, b = subs
    load(0, a)
    for n in range(n_in):
        in_proj(0, a, n)
    load(1, b)
    in_proj(1, b, 0)
    conv(0, a)
    in_proj(1, b, 1)
    pool(0, a)
    in_proj(1, b, 2)
    pool_map(0, a)
    in_proj(1, b, 3)
    out_proj(0, a, 0)
    conv(1, b)
    out_proj(0, a, 1)
    pool(1, b)
    pool_map(1, b)
    out_proj(1, b, 0)
    norm1(0, a)
    out_proj(1, b, 1)
    mlp_in(0, a, 0)
    norm1(1, b)
    for c in range(1, n_ff):
        mlp_in(0, a, c)
    mlp_out(0, a)
    mlp_in(1, b, 0)
    norm2(0, a)
    for c in range(1, n_ff):
        mlp_in(1, b, c)
    mlp_out(1, b, parts=LAST_OUT_PARTS)
    norm2(1, b)


def _const_spec(shape):
    return pl.BlockSpec(shape, lambda b, j: (0,) * len(shape), pipeline_mode=pl.Buffered(1))


def _layer(x, c, w_ada, b_ada, w_in, conv_w, w_pool, pool_scale, w_out, ln1_g, ln1_b, w1, w2,
           ln2_g, ln2_b, *, alpha, layer):
    batch, seq, d = x.shape
    conv_ch = conv_w.shape[1]
    n_groups, group_dim, _ = w_pool.shape
    pool_width = n_groups * group_dim
    tm, sub = TOKEN_TILE, SUB_TILE
    assert seq % tm == 0 and tm % sub == 0 and sub >= POOL_HALO and batch <= MOD_ROWS
    assert max(POOL_WINDOWS) <= POOL_HALO and CONV_K - 1 <= CONV_HALO
    assert all(win == 2 ** (g + 1) for g, win in enumerate(POOL_WINDOWS))
    assert n_groups % POOL_GROUPS_PER_TILE == 0 and w1.shape[2] % FF_CHUNK == 0
    assert conv_ch == COL_PIECE and pool_width == COL_PIECE and w_in.shape[2] % COL_PIECE == 0
    streamed = (w_ada, w_in, w_out, w1, w2)
    assert all(w.shape[1] % STAGE_ROWS == 0 and w.shape[2] % STAGE_COLS == 0 for w in streamed)
    bf16 = jnp.bfloat16
    row = lambda a: a.reshape(1, -1)
    operands = (x, c, w_ada, row(b_ada), w_in, conv_w, w_pool, row(pool_scale), w_out,
                row(ln1_g), row(ln1_b), w1, w2, row(ln2_g), row(ln2_b))
    in_specs = [pl.BlockSpec((1, tm, d), lambda b, j: (b, j, 0))]
    in_specs += [pl.BlockSpec(memory_space=pl.ANY) if any(a is w for w in streamed)
                 else _const_spec(a.shape) for a in operands[1:]]
    return pl.pallas_call(
        functools.partial(_layer_kernel, alpha=alpha, sub=sub, layer=layer),
        grid=(batch, seq // tm),
        in_specs=in_specs,
        out_specs=pl.BlockSpec((1, tm, d), lambda b, j: (b, j, 0)),
        out_shape=jax.ShapeDtypeStruct(x.shape, x.dtype),
        scratch_shapes=[
            pltpu.VMEM((CONV_HALO, conv_ch), jnp.float32),
            pltpu.VMEM((POOL_HALO, pool_width), jnp.float32),
            pltpu.VMEM((n_groups // POOL_GROUPS_PER_TILE,
                        POOL_GROUPS_PER_TILE * group_dim, POOL_GROUPS_PER_TILE * group_dim), bf16),
            pltpu.VMEM((MOD_ROWS, w_ada.shape[2]), jnp.float32),
        ] + [pltpu.VMEM(w.shape[1:], bf16) for w in streamed[1:]] + [
            pltpu.VMEM((STAGE_SLOTS, STAGE_ROWS, STAGE_COLS), jnp.float32),
            pltpu.SemaphoreType.DMA((STAGE_SLOTS,)),
        ],
        compiler_params=pltpu.CompilerParams(
            dimension_semantics=("arbitrary", "arbitrary"),
            vmem_limit_bytes=VMEM_LIMIT_BYTES),
        name="hybrid_layer",
    )(*operands)


def kernel(x, c, w_ada, b_ada, w_in, conv_w, w_pool, pool_scale, w_out, ln1_g, ln1_b, w_mlp_in,
           w_mlp_out, ln2_g, ln2_b):
    depth = w_in.shape[0]
    alpha = (2.0 * depth) ** 0.25
    for l in range(depth):
        x = _layer(x, c, w_ada, b_ada[l], w_in, conv_w[l], w_pool[l], pool_scale[l], w_out,
                   ln1_g[l], ln1_b[l], w_mlp_in, w_mlp_out, ln2_g[l], ln2_b[l], alpha=alpha,
                   layer=l)
    return x
```

```python
import functools

import jax
import jax.numpy as jnp
from jax import lax
from jax.experimental import pallas as pl
from jax.experimental.pallas import tpu as pltpu

CONV_K = 3
POOL_WINDOWS = (2, 4, 8, 16)
N_MOD = 6
LN_EPS = 1e-5

CONV_HALO = 8
POOL_HALO = 16
MOD_ROWS = 8

TOKEN_TILE = 1024
SUB_TILE = 512
COL_PIECE = 512
FF_CHUNK = 1024
LAST_OUT_PARTS = 2
STAGE_SLOTS = 3
STAGE_ROWS = 512
STAGE_COLS = 1024
VMEM_LIMIT_BYTES = 60 * 1024 * 1024


def _layer_norm(r, g, b):
    mu = jnp.mean(r, axis=-1, keepdims=True)
    d = r - mu
    var = jnp.mean(d * d, axis=-1, keepdims=True)
    return d * lax.rsqrt(var + LN_EPS) * g + b


def _dot(a, b):
    return jnp.dot(a, b, preferred_element_type=jnp.float32)


def _layer_kernel(x_ref, c_ref, w_ada_hbm, b_ada_ref, w_in_hbm, conv_w_ref, w_pool_ref,
                  pool_scale_ref, w_out_hbm, ln1_g_ref, ln1_b_ref, w1_hbm, w2_hbm, ln2_g_ref,
                  ln2_b_ref, o_ref,
                  carry_u, carry_v, w_comb, mod_s, w_in_ref, w_out_ref, w1_ref, w2_ref, stage,
                  stage_sem, *, alpha, sub, layer):
    tm = x_ref.shape[1]
    d = x_ref.shape[2]
    n_sub = tm // sub
    conv_ch = conv_w_ref.shape[1]
    n_groups, group_dim, _ = w_pool_ref.shape
    d_ff = w1_ref.shape[1]
    j = pl.program_id(1)

    @pl.when(j == 0)
    def _():
        carry_u[...] = jnp.zeros(carry_u.shape, carry_u.dtype)
        carry_v[...] = jnp.zeros(carry_v.shape, carry_v.dtype)

    @pl.when((pl.program_id(0) == 0) & (j == 0))
    def _():
        c = c_ref[...]
        cond = c * jax.nn.sigmoid(c)
        cond = jnp.concatenate(
            [cond, jnp.zeros((MOD_ROWS - c.shape[0], d), cond.dtype)], axis=0).astype(jnp.bfloat16)

        def into_mod(r0, c0, block):
            part = _dot(cond[:, r0:r0 + STAGE_ROWS], block.astype(jnp.bfloat16))
            base = b_ada_ref[:, c0:c0 + STAGE_COLS] if r0 == 0 else mod_s[:, c0:c0 + STAGE_COLS]
            mod_s[:, c0:c0 + STAGE_COLS] = base + part

        def into(dst):
            def keep(r0, c0, block):
                dst[r0:r0 + STAGE_ROWS, c0:c0 + STAGE_COLS] = block.astype(dst.dtype)
            return keep

        def into_out_proj(r0, c0, block):
            into(w_out_ref)(r0, c0, block)
            for g in range(n_groups):
                g0 = conv_ch + g * group_dim
                if r0 <= g0 and g0 + group_dim <= r0 + STAGE_ROWS:
                    scaled = w_pool_ref[g] * pool_scale_ref[:, g * group_dim:(g + 1) * group_dim]
                    w_comb[g * group_dim:(g + 1) * group_dim, c0:c0 + STAGE_COLS] = jnp.dot(
                        scaled, block[g0 - r0:g0 - r0 + group_dim, :],
                        precision=lax.Precision.HIGHEST,
                        preferred_element_type=jnp.float32).astype(w_comb.dtype)

        blocks = [(hbm, r0, c0, consume)
                  for hbm, shape, consume in (
                      (w_ada_hbm, (d, mod_s.shape[1]), into_mod),
                      (w_in_hbm, w_in_ref.shape, into(w_in_ref)),
                      (w_out_hbm, w_out_ref.shape, into_out_proj),
                      (w1_hbm, w1_ref.shape, into(w1_ref)),
                      (w2_hbm, w2_ref.shape, into(w2_ref)))
                  for r0 in range(0, shape[0], STAGE_ROWS)
                  for c0 in range(0, shape[1], STAGE_COLS)]

        def fetch(i):
            hbm, r0, c0, _ = blocks[i]
            return pltpu.make_async_copy(
                hbm.at[layer, pl.ds(r0, STAGE_ROWS), pl.ds(c0, STAGE_COLS)],
                stage.at[i % STAGE_SLOTS], stage_sem.at[i % STAGE_SLOTS])

        ahead = STAGE_SLOTS - 1
        for i in range(min(ahead, len(blocks))):
            fetch(i).start()
        for i, (_, r0, c0, consume) in enumerate(blocks):
            if i + ahead < len(blocks):
                fetch(i + ahead).start()
            fetch(i).wait()
            consume(r0, c0, stage[i % STAGE_SLOTS])

    mod = mod_s[pl.ds(pl.program_id(0), 1), :]
    sh1, sc1, g1, sh2, sc2, g2 = (mod[:, i * d:(i + 1) * d] for i in range(N_MOD))
    cw = conv_w_ref[...]

    def load(s, st):
        st["x"] = x_ref[0, pl.ds(s * sub, sub), :]
        st["h"] = (st["x"] * (1.0 + sc1) + sh1).astype(jnp.bfloat16)
        st["z"] = {}

    def in_proj(s, st, n):
        st["z"][n] = _dot(st["h"], w_in_ref[:, n * COL_PIECE:(n + 1) * COL_PIECE])

    def with_history(s, st, key, cur, carry, halo):
        st[key] = cur[sub - halo:, :]
        if s == n_sub - 1:
            carry[...] = st[key]
        return jnp.concatenate([carry[...] if s == 0 else subs[s - 1][key], cur], axis=0)

    def conv(s, st):
        u = st["z"][1] * st["z"][2]
        ext = with_history(s, st, "u_tail", u, carry_u, CONV_HALO)
        acc = u * cw[CONV_K - 1:CONV_K]
        for k in range(CONV_K - 1):
            shifted = pltpu.roll(ext, CONV_K - 1 - k, axis=0)[CONV_HALO:, :]
            acc = acc + shifted * cw[k:k + 1]
        st["y_conv"] = (st["z"][0] * acc).astype(jnp.bfloat16)

    def pool(s, st):
        v_pool = st["z"][3]
        ext = with_history(s, st, "v_tail", v_pool, carry_v, POOL_HALO)
        sums, part = [], ext
        for g, win in enumerate(POOL_WINDOWS):
            part = part + pltpu.roll(part, win // 2, axis=0)
            sums.append(part[POOL_HALO:, 0:group_dim])
            part = part[:, group_dim:]
        pos = (j * tm + s * sub + 1
               + lax.broadcasted_iota(jnp.int32, (sub, 1), 0)).astype(jnp.float32)
        pooled = [
            sums[g] / jnp.minimum(pos, float(win)) - v_pool[:, g * group_dim:(g + 1) * group_dim]
            for g, win in enumerate(POOL_WINDOWS)]
        st["p_in"] = jnp.concatenate(pooled, axis=1).astype(jnp.bfloat16)

    def out_proj(s, st, k):
        if k == 0:
            st["mix"] = _dot(st["y_conv"], w_out_ref[0:conv_ch, :])
        else:
            st["mix"] = st["mix"] + _dot(st["p_in"], w_comb[...])

    def norm1(s, st):
        x1 = _layer_norm(alpha * st["x"] + (1.0 + g1) * st["mix"], ln1_g_ref[...], ln1_b_ref[...])
        st["x1"] = x1
        st["h2"] = (x1 * (1.0 + sc2) + sh2).astype(jnp.bfloat16)

    def mlp_in(s, st, c):
        c0 = c * FF_CHUNK
        hid = _dot(st["h2"], w1_ref[:, c0:c0 + FF_CHUNK])
        st.setdefault("hid", []).append(jnp.square(jnp.maximum(hid, 0.0)).astype(jnp.bfloat16))

    def mlp_out(s, st, parts=1):
        hid = jnp.concatenate(st["hid"], axis=1)
        rows = sub // parts
        st["f"] = [_dot(hid[r:r + rows, :], w2_ref[...]) for r in range(0, sub, rows)]

    def norm2(s, st):
        rows = sub // len(st["f"])
        for i, f in enumerate(st["f"]):
            r = slice(i * rows, (i + 1) * rows)
            o_ref[0, pl.ds(s * sub + i * rows, rows), :] = _layer_norm(
                alpha * st["x1"][r] + (1.0 + g2) * f, ln2_g_ref[...], ln2_b_ref[...])

    n_in = w_in_ref.shape[1] // COL_PIECE
    n_ff = d_ff // FF_CHUNK

    assert n_sub == 2 and n_in == 4
    subs = [{} for _ in range(n_sub)]
    a, b = subs
    load(0, a)
    for n in range(n_in):
        in_proj(0, a, n)
    load(1, b)
    in_proj(1, b, 0)
    conv(0, a)
    in_proj(1, b, 1)
    pool(0, a)
    in_proj(1, b, 2)
    in_proj(1, b, 3)
    out_proj(0, a, 0)
    conv(1, b)
    out_proj(0, a, 1)
    pool(1, b)
    out_proj(1, b, 0)
    norm1(0, a)
    out_proj(1, b, 1)
    mlp_in(0, a, 0)
    norm1(1, b)
    for c in range(1, n_ff):
        mlp_in(0, a, c)
    mlp_out(0, a)
    mlp_in(1, b, 0)
    norm2(0, a)
    for c in range(1, n_ff):
        mlp_in(1, b, c)
    mlp_out(1, b, parts=LAST_OUT_PARTS)
    norm2(1, b)


def _const_spec(shape):
    return pl.BlockSpec(shape, lambda b, j: (0,) * len(shape), pipeline_mode=pl.Buffered(1))


def _layer(x, c, w_ada, b_ada, w_in, conv_w, w_pool, pool_scale, w_out, ln1_g, ln1_b, w1, w2,
           ln2_g, ln2_b, *, alpha, layer):
    batch, seq, d = x.shape
    conv_ch = conv_w.shape[1]
    n_groups, group_dim, _ = w_pool.shape
    pool_width = n_groups * group_dim
    tm, sub = TOKEN_TILE, SUB_TILE
    assert seq % tm == 0 and tm % sub == 0 and sub >= POOL_HALO and batch <= MOD_ROWS
    assert max(POOL_WINDOWS) <= POOL_HALO and CONV_K - 1 <= CONV_HALO
    assert all(win == 2 ** (g + 1) for g, win in enumerate(POOL_WINDOWS))
    assert w1.shape[2] % FF_CHUNK == 0 and STAGE_ROWS % group_dim == 0
    assert conv_ch == COL_PIECE and pool_width == COL_PIECE and w_in.shape[2] % COL_PIECE == 0
    streamed = (w_ada, w_in, w_out, w1, w2)
    assert all(w.shape[1] % STAGE_ROWS == 0 and w.shape[2] % STAGE_COLS == 0 for w in streamed)
    bf16 = jnp.bfloat16
    row = lambda a: a.reshape(1, -1)
    operands = (x, c, w_ada, row(b_ada), w_in, conv_w, w_pool, row(pool_scale), w_out,
                row(ln1_g), row(ln1_b), w1, w2, row(ln2_g), row(ln2_b))
    in_specs = [pl.BlockSpec((1, tm, d), lambda b, j: (b, j, 0))]
    in_specs += [pl.BlockSpec(memory_space=pl.ANY) if any(a is w for w in streamed)
                 else _const_spec(a.shape) for a in operands[1:]]
    return pl.pallas_call(
        functools.partial(_layer_kernel, alpha=alpha, sub=sub, layer=layer),
        grid=(batch, seq // tm),
        in_specs=in_specs,
        out_specs=pl.BlockSpec((1, tm, d), lambda b, j: (b, j, 0)),
        out_shape=jax.ShapeDtypeStruct(x.shape, x.dtype),
        scratch_shapes=[
            pltpu.VMEM((CONV_HALO, conv_ch), jnp.float32),
            pltpu.VMEM((POOL_HALO, pool_width), jnp.float32),
            pltpu.VMEM((pool_width, d), bf16),
            pltpu.VMEM((MOD_ROWS, w_ada.shape[2]), jnp.float32),
        ] + [pltpu.VMEM(w.shape[1:], bf16) for w in streamed[1:]] + [
            pltpu.VMEM((STAGE_SLOTS, STAGE_ROWS, STAGE_COLS), jnp.float32),
            pltpu.SemaphoreType.DMA((STAGE_SLOTS,)),
        ],
        compiler_params=pltpu.CompilerParams(
            dimension_semantics=("arbitrary", "arbitrary"),
            vmem_limit_bytes=VMEM_LIMIT_BYTES),
        name="hybrid_layer",
    )(*operands)


def kernel(x, c, w_ada, b_ada, w_in, conv_w, w_pool, pool_scale, w_out, ln1_g, ln1_b, w_mlp_in,
           w_mlp_out, ln2_g, ln2_b):
    depth = w_in.shape[0]
    alpha = (2.0 * depth) ** 0.25
    for l in range(depth):
        x = _layer(x, c, w_ada, b_ada[l], w_in, conv_w[l], w_pool[l], pool_scale[l], w_out,
                   ln1_g[l], ln1_b[l], w_mlp_in, w_mlp_out, ln2_g[l], ln2_b[l], alpha=alpha,
                   layer=l)
    return x
```

```python
import functools

import jax
import jax.numpy as jnp
from jax import lax
from jax.experimental import pallas as pl
from jax.experimental.pallas import tpu as pltpu

CONV_K = 3
POOL_WINDOWS = (2, 4, 8, 16)
N_MOD = 6
LN_EPS = 1e-5

CONV_HALO = 8
POOL_HALO = 16
MOD_ROWS = 8

TOKEN_TILE = 1024
SUB_TILE = 512
COL_PIECE = 1024
FF_CHUNK = 1024
LAST_OUT_PARTS = 2
STAGE_SLOTS = 4
STAGE_ROWS = 512
STAGE_COLS = 1024
VMEM_LIMIT_BYTES = 60 * 1024 * 1024


def _layer_norm(r, g, b):
    mu = jnp.mean(r, axis=-1, keepdims=True)
    d = r - mu
    var = jnp.mean(d * d, axis=-1, keepdims=True)
    return d * lax.rsqrt(var + LN_EPS) * g + b


def _dot(a, b):
    return jnp.dot(a, b, preferred_element_type=jnp.float32)


def _layer_kernel(x_ref, c_ref, w_ada_hbm, b_ada_ref, w_in_hbm, conv_w_ref, w_pool_ref,
                  pool_scale_ref, w_out_hbm, ln1_g_ref, ln1_b_ref, w1_hbm, w2_hbm, ln2_g_ref,
                  ln2_b_ref, o_ref,
                  carry_u, carry_v, w_comb, mod_s, w_in_ref, w_out_ref, w1_ref, w2_ref, stage,
                  stage_sem, *, alpha, sub, layer):
    tm = x_ref.shape[1]
    d = x_ref.shape[2]
    n_sub = tm // sub
    conv_ch = conv_w_ref.shape[1]
    n_groups, group_dim, _ = w_pool_ref.shape
    d_ff = w1_ref.shape[1]
    j = pl.program_id(1)

    @pl.when(j == 0)
    def _():
        carry_u[...] = jnp.zeros(carry_u.shape, carry_u.dtype)
        carry_v[...] = jnp.zeros(carry_v.shape, carry_v.dtype)

    @pl.when((pl.program_id(0) == 0) & (j == 0))
    def _():
        c = c_ref[...]
        cond = c * jax.nn.sigmoid(c)
        cond = jnp.concatenate(
            [cond, jnp.zeros((MOD_ROWS - c.shape[0], d), cond.dtype)], axis=0).astype(jnp.bfloat16)

        def into_mod(r0, c0, block):
            part = _dot(cond[:, r0:r0 + STAGE_ROWS], block.astype(jnp.bfloat16))
            base = b_ada_ref[:, c0:c0 + STAGE_COLS] if r0 == 0 else mod_s[:, c0:c0 + STAGE_COLS]
            mod_s[:, c0:c0 + STAGE_COLS] = base + part

        def into(dst):
            def keep(r0, c0, block):
                dst[r0:r0 + STAGE_ROWS, c0:c0 + STAGE_COLS] = block.astype(dst.dtype)
            return keep

        def into_out_proj(r0, c0, block):
            if r0 < conv_ch:
                into(w_out_ref)(r0, c0, block)
            for g in range(n_groups):
                g0 = conv_ch + g * group_dim
                if r0 <= g0 and g0 + group_dim <= r0 + STAGE_ROWS:
                    scaled = w_pool_ref[g] * pool_scale_ref[:, g * group_dim:(g + 1) * group_dim]
                    w_comb[g * group_dim:(g + 1) * group_dim, c0:c0 + STAGE_COLS] = jnp.dot(
                        scaled, block[g0 - r0:g0 - r0 + group_dim, :],
                        precision=lax.Precision.HIGHEST,
                        preferred_element_type=jnp.float32).astype(w_comb.dtype)

        blocks = [(hbm, r0, c0, consume)
                  for hbm, shape, consume in (
                      (w_ada_hbm, (d, mod_s.shape[1]), into_mod),
                      (w_in_hbm, w_in_ref.shape, into(w_in_ref)),
                      (w_out_hbm, (d, d), into_out_proj),
                      (w1_hbm, w1_ref.shape, into(w1_ref)),
                      (w2_hbm, w2_ref.shape, into(w2_ref)))
                  for r0 in range(0, shape[0], STAGE_ROWS)
                  for c0 in range(0, shape[1], STAGE_COLS)]

        def fetch(i):
            hbm, r0, c0, _ = blocks[i]
            return pltpu.make_async_copy(
                hbm.at[layer, pl.ds(r0, STAGE_ROWS), pl.ds(c0, STAGE_COLS)],
                stage.at[i % STAGE_SLOTS], stage_sem.at[i % STAGE_SLOTS])

        ahead = STAGE_SLOTS - 1
        for i in range(min(ahead, len(blocks))):
            fetch(i).start()
        for i, (_, r0, c0, consume) in enumerate(blocks):
            if i + ahead < len(blocks):
                fetch(i + ahead).start()
            fetch(i).wait()
            consume(r0, c0, stage[i % STAGE_SLOTS])

    mod = mod_s[pl.ds(pl.program_id(0), 1), :]
    sh1, sc1, g1, sh2, sc2, g2 = (mod[:, i * d:(i + 1) * d] for i in range(N_MOD))
    cw = conv_w_ref[...]

    def load(s, st):
        x = x_ref[0, pl.ds(s * sub, sub), :]
        st["h"] = (x * (1.0 + sc1) + sh1).astype(jnp.bfloat16)
        st["z"] = {}

    def in_proj(s, st, n):
        piece = _dot(st["h"], w_in_ref[:, n * COL_PIECE:(n + 1) * COL_PIECE])
        for k in range(COL_PIECE // conv_ch):
            st["z"][n * (COL_PIECE // conv_ch) + k] = piece[:, k * conv_ch:(k + 1) * conv_ch]

    def with_history(s, st, key, cur, carry, halo):
        st[key] = cur[sub - halo:, :]
        if s == n_sub - 1:
            carry[...] = st[key]
        return jnp.concatenate([carry[...] if s == 0 else subs[s - 1][key], cur], axis=0)

    def conv(s, st):
        u = st["z"][1] * st["z"][2]
        ext = with_history(s, st, "u_tail", u, carry_u, CONV_HALO)
        acc = u * cw[CONV_K - 1:CONV_K]
        for k in range(CONV_K - 1):
            shifted = pltpu.roll(ext, CONV_K - 1 - k, axis=0)[CONV_HALO:, :]
            acc = acc + shifted * cw[k:k + 1]
        st["y_conv"] = (st["z"][0] * acc).astype(jnp.bfloat16)

    def pool(s, st):
        v_pool = st["z"][3]
        ext = with_history(s, st, "v_tail", v_pool, carry_v, POOL_HALO)
        sums, part = [], ext
        for g, win in enumerate(POOL_WINDOWS):
            part = part + pltpu.roll(part, win // 2, axis=0)
            sums.append(part[POOL_HALO:, 0:group_dim])
            part = part[:, group_dim:]
        pos = (j * tm + s * sub + 1
               + lax.broadcasted_iota(jnp.int32, (sub, 1), 0)).astype(jnp.float32)
        pooled = [
            sums[g] / jnp.minimum(pos, float(win)) - v_pool[:, g * group_dim:(g + 1) * group_dim]
            for g, win in enumerate(POOL_WINDOWS)]
        st["p_in"] = jnp.concatenate(pooled, axis=1).astype(jnp.bfloat16)

    def out_proj(s, st, k):
        if k == 0:
            st["mix"] = _dot(st["y_conv"], w_out_ref[0:conv_ch, :])
        else:
            st["mix"] = st["mix"] + _dot(st["p_in"], w_comb[...])

    def norm1(s, st):
        x = x_ref[0, pl.ds(s * sub, sub), :]
        x1 = _layer_norm(alpha * x + (1.0 + g1) * st["mix"], ln1_g_ref[...], ln1_b_ref[...])
        st["x1"] = x1
        st["h2"] = (x1 * (1.0 + sc2) + sh2).astype(jnp.bfloat16)

    def mlp_in(s, st, c):
        c0 = c * FF_CHUNK
        hid = _dot(st["h2"], w1_ref[:, c0:c0 + FF_CHUNK])
        st.setdefault("hid", []).append(jnp.square(jnp.maximum(hid, 0.0)).astype(jnp.bfloat16))

    def mlp_out(s, st, parts=1):
        hid = jnp.concatenate(st["hid"], axis=1)
        rows = sub // parts
        st["f"] = [_dot(hid[r:r + rows, :], w2_ref[...]) for r in range(0, sub, rows)]

    def norm2(s, st):
        rows = sub // len(st["f"])
        for i, f in enumerate(st["f"]):
            r = slice(i * rows, (i + 1) * rows)
            o_ref[0, pl.ds(s * sub + i * rows, rows), :] = _layer_norm(
                alpha * st["x1"][r] + (1.0 + g2) * f, ln2_g_ref[...], ln2_b_ref[...])

    n_in = w_in_ref.shape[1] // COL_PIECE
    n_ff = d_ff // FF_CHUNK

    assert n_sub == 2 and n_in in (2, 4)
    subs = [{} for _ in range(n_sub)]
    a, b = subs
    load(0, a)
    for n in range(n_in):
        in_proj(0, a, n)
    load(1, b)
    in_proj(1, b, 0)
    conv(0, a)
    if n_in == 4:
        in_proj(1, b, 1)
    pool(0, a)
    for n in range(n_in // 2, n_in):
        in_proj(1, b, n)
    out_proj(0, a, 0)
    conv(1, b)
    out_proj(0, a, 1)
    pool(1, b)
    out_proj(1, b, 0)
    norm1(0, a)
    out_proj(1, b, 1)
    mlp_in(0, a, 0)
    norm1(1, b)
    for c in range(1, n_ff):
        mlp_in(0, a, c)
    mlp_out(0, a)
    mlp_in(1, b, 0)
    norm2(0, a)
    for c in range(1, n_ff):
        mlp_in(1, b, c)
    mlp_out(1, b, parts=LAST_OUT_PARTS)
    norm2(1, b)


def _const_spec(shape):
    return pl.BlockSpec(shape, lambda b, j: (0,) * len(shape), pipeline_mode=pl.Buffered(1))


def _layer(x, c, w_ada, b_ada, w_in, conv_w, w_pool, pool_scale, w_out, ln1_g, ln1_b, w1, w2,
           ln2_g, ln2_b, *, alpha, layer):
    batch, seq, d = x.shape
    conv_ch = conv_w.shape[1]
    n_groups, group_dim, _ = w_pool.shape
    pool_width = n_groups * group_dim
    tm, sub = TOKEN_TILE, SUB_TILE
    assert seq % tm == 0 and tm % sub == 0 and sub >= POOL_HALO and batch <= MOD_ROWS
    assert max(POOL_WINDOWS) <= POOL_HALO and CONV_K - 1 <= CONV_HALO
    assert all(win == 2 ** (g + 1) for g, win in enumerate(POOL_WINDOWS))
    assert w1.shape[2] % FF_CHUNK == 0 and STAGE_ROWS % group_dim == 0
    assert COL_PIECE % conv_ch == 0 and pool_width == conv_ch and w_in.shape[2] % COL_PIECE == 0
    streamed = (w_ada, w_in, w_out, w1, w2)
    assert all(w.shape[1] % STAGE_ROWS == 0 and w.shape[2] % STAGE_COLS == 0 for w in streamed)
    bf16 = jnp.bfloat16
    row = lambda a: a.reshape(1, -1)
    operands = (x, c, w_ada, row(b_ada), w_in, conv_w, w_pool, row(pool_scale), w_out,
                row(ln1_g), row(ln1_b), w1, w2, row(ln2_g), row(ln2_b))
    in_specs = [pl.BlockSpec((1, tm, d), lambda b, j: (b, j, 0))]
    in_specs += [pl.BlockSpec(memory_space=pl.ANY) if any(a is w for w in streamed)
                 else _const_spec(a.shape) for a in operands[1:]]
    return pl.pallas_call(
        functools.partial(_layer_kernel, alpha=alpha, sub=sub, layer=layer),
        grid=(batch, seq // tm),
        in_specs=in_specs,
        out_specs=pl.BlockSpec((1, tm, d), lambda b, j: (b, j, 0)),
        out_shape=jax.ShapeDtypeStruct(x.shape, x.dtype),
        scratch_shapes=[
            pltpu.VMEM((CONV_HALO, conv_ch), jnp.float32),
            pltpu.VMEM((POOL_HALO, pool_width), jnp.float32),
            pltpu.VMEM((pool_width, d), bf16),
            pltpu.VMEM((MOD_ROWS, w_ada.shape[2]), jnp.float32),
            pltpu.VMEM(w_in.shape[1:], bf16),
            pltpu.VMEM((conv_ch, d), bf16),
            pltpu.VMEM(w1.shape[1:], bf16),
            pltpu.VMEM(w2.shape[1:], bf16),
            pltpu.VMEM((STAGE_SLOTS, STAGE_ROWS, STAGE_COLS), jnp.float32),
            pltpu.SemaphoreType.DMA((STAGE_SLOTS,)),
        ],
        compiler_params=pltpu.CompilerParams(
            dimension_semantics=("arbitrary", "arbitrary"),
            vmem_limit_bytes=VMEM_LIMIT_BYTES),
        name="hybrid_layer",
    )(*operands)


def kernel(x, c, w_ada, b_ada, w_in, conv_w, w_pool, pool_scale, w_out, ln1_g, ln1_b, w_mlp_in,
           w_mlp_out, ln2_g, ln2_b):
    depth = w_in.shape[0]
    alpha = (2.0 * depth) ** 0.25
    for l in range(depth):
        x = _layer(x, c, w_ada, b_ada[l], w_in, conv_w[l], w_pool[l], pool_scale[l], w_out,
                   ln1_g[l], ln1_b[l], w_mlp_in, w_mlp_out, ln2_g[l], ln2_b[l], alpha=alpha,
                   layer=l)
    return x
```

```python
import functools

import jax
import jax.numpy as jnp
from jax import lax
from jax.experimental import pallas as pl
from jax.experimental.pallas import tpu as pltpu

CONV_K = 3
POOL_WINDOWS = (2, 4, 8, 16)
N_MOD = 6
LN_EPS = 1e-5

CONV_HALO = 8
POOL_HALO = 16
MOD_ROWS = 8

TOKEN_TILE = 1024
SUB_TILE = 512
COL_PIECE = 1024
FF_CHUNK = 1024
LAST_OUT_PARTS = 2
STAGE_SLOTS = 4
STAGE_ROWS = 512
STAGE_COLS = 1024
VMEM_LIMIT_BYTES = 60 * 1024 * 1024


def _layer_norm(r, g, b):
    mu = jnp.mean(r, axis=-1, keepdims=True)
    d = r - mu
    var = jnp.mean(d * d, axis=-1, keepdims=True)
    return d * lax.rsqrt(var + LN_EPS) * g + b


def _dot(a, b):
    return jnp.dot(a, b, preferred_element_type=jnp.float32)


def _layer_kernel(x_ref, c_ref, w_ada_hbm, b_ada_ref, w_in_hbm, conv_w_ref, w_pool_ref,
                  pool_scale_ref, w_out_hbm, ln1_g_ref, ln1_b_ref, w1_hbm, w2_hbm, ln2_g_ref,
                  ln2_b_ref, o_ref,
                  carry_u, carry_v, w_comb, mod_s, w_in_ref, w_out_ref, w1_ref, w2_ref, stage,
                  stage_sem, *, alpha, sub, layer):
    tm = x_ref.shape[1]
    d = x_ref.shape[2]
    n_sub = tm // sub
    conv_ch = conv_w_ref.shape[1]
    n_groups, group_dim, _ = w_pool_ref.shape
    d_ff = w1_ref.shape[1]
    j = pl.program_id(1)

    @pl.when(j == 0)
    def _():
        carry_u[...] = jnp.zeros(carry_u.shape, carry_u.dtype)
        carry_v[...] = jnp.zeros(carry_v.shape, carry_v.dtype)

    @pl.when((pl.program_id(0) == 0) & (j == 0))
    def _():
        c = c_ref[...]
        cond = c * jax.nn.sigmoid(c)
        cond = jnp.concatenate(
            [cond, jnp.zeros((MOD_ROWS - c.shape[0], d), cond.dtype)], axis=0).astype(jnp.bfloat16)

        def into_mod(r0, c0, block):
            part = _dot(cond[:, r0:r0 + STAGE_ROWS], block.astype(jnp.bfloat16))
            base = b_ada_ref[:, c0:c0 + STAGE_COLS] if r0 == 0 else mod_s[:, c0:c0 + STAGE_COLS]
            mod_s[:, c0:c0 + STAGE_COLS] = base + part

        def into(dst):
            def keep(r0, c0, block):
                dst[r0:r0 + STAGE_ROWS, c0:c0 + STAGE_COLS] = block.astype(dst.dtype)
            return keep

        def into_out_proj(r0, c0, block):
            if r0 < conv_ch:
                into(w_out_ref)(r0, c0, block)
            for g in range(n_groups):
                g0 = conv_ch + g * group_dim
                if r0 <= g0 and g0 + group_dim <= r0 + STAGE_ROWS:
                    scaled = w_pool_ref[g] * pool_scale_ref[:, g * group_dim:(g + 1) * group_dim]
                    w_comb[g * group_dim:(g + 1) * group_dim, c0:c0 + STAGE_COLS] = jnp.dot(
                        scaled, block[g0 - r0:g0 - r0 + group_dim, :],
                        precision=lax.Precision.HIGHEST,
                        preferred_element_type=jnp.float32).astype(w_comb.dtype)

        blocks = [(hbm, r0, c0, consume)
                  for hbm, shape, consume in (
                      (w_ada_hbm, (d, mod_s.shape[1]), into_mod),
                      (w_in_hbm, w_in_ref.shape, into(w_in_ref)),
                      (w_out_hbm, (d, d), into_out_proj),
                      (w1_hbm, w1_ref.shape, into(w1_ref)),
                      (w2_hbm, w2_ref.shape, into(w2_ref)))
                  for r0 in range(0, shape[0], STAGE_ROWS)
                  for c0 in range(0, shape[1], STAGE_COLS)]

        def fetch(i):
            hbm, r0, c0, _ = blocks[i]
            return pltpu.make_async_copy(
                hbm.at[layer, pl.ds(r0, STAGE_ROWS), pl.ds(c0, STAGE_COLS)],
                stage.at[i % STAGE_SLOTS], stage_sem.at[i % STAGE_SLOTS])

        ahead = STAGE_SLOTS - 1
        for i in range(min(ahead, len(blocks))):
            fetch(i).start()
        for i, (_, r0, c0, consume) in enumerate(blocks):
            if i + ahead < len(blocks):
                fetch(i + ahead).start()
            fetch(i).wait()
            consume(r0, c0, stage[i % STAGE_SLOTS])

    mod = mod_s[pl.ds(pl.program_id(0), 1), :]
    sh1, sc1, g1, sh2, sc2, g2 = (mod[:, i * d:(i + 1) * d] for i in range(N_MOD))
    cw = conv_w_ref[...]

    def load(s, st):
        x = x_ref[0, pl.ds(s * sub, sub), :]
        st["h"] = (x * (1.0 + sc1) + sh1).astype(jnp.bfloat16)
        st["z"] = {}

    def in_proj(s, st, n):
        piece = _dot(st["h"], w_in_ref[:, n * COL_PIECE:(n + 1) * COL_PIECE])
        for k in range(COL_PIECE // conv_ch):
            st["z"][n * (COL_PIECE // conv_ch) + k] = piece[:, k * conv_ch:(k + 1) * conv_ch]

    def in_proj_both(n):
        piece = _dot(jnp.concatenate([st["h"] for st in subs], axis=0),
                     w_in_ref[:, n * COL_PIECE:(n + 1) * COL_PIECE])
        for s, st in enumerate(subs):
            for k in range(COL_PIECE // conv_ch):
                st["z"][n * (COL_PIECE // conv_ch) + k] = (
                    piece[s * sub:(s + 1) * sub, k * conv_ch:(k + 1) * conv_ch])

    def with_history(s, st, key, cur, carry, halo):
        st[key] = cur[sub - halo:, :]
        if s == n_sub - 1:
            carry[...] = st[key]
        return jnp.concatenate([carry[...] if s == 0 else subs[s - 1][key], cur], axis=0)

    def conv(s, st):
        u = st["z"][1] * st["z"][2]
        ext = with_history(s, st, "u_tail", u, carry_u, CONV_HALO)
        acc = u * cw[CONV_K - 1:CONV_K]
        for k in range(CONV_K - 1):
            shifted = pltpu.roll(ext, CONV_K - 1 - k, axis=0)[CONV_HALO:, :]
            acc = acc + shifted * cw[k:k + 1]
        st["y_conv"] = (st["z"][0] * acc).astype(jnp.bfloat16)

    def pool(s, st):
        v_pool = st["z"][3]
        ext = with_history(s, st, "v_tail", v_pool, carry_v, POOL_HALO)
        sums, part = [], ext
        for g, win in enumerate(POOL_WINDOWS):
            part = part + pltpu.roll(part, win // 2, axis=0)
            sums.append(part[POOL_HALO:, 0:group_dim])
            part = part[:, group_dim:]
        pos = (j * tm + s * sub + 1
               + lax.broadcasted_iota(jnp.int32, (sub, 1), 0)).astype(jnp.float32)
        pooled = [
            sums[g] / jnp.minimum(pos, float(win)) - v_pool[:, g * group_dim:(g + 1) * group_dim]
            for g, win in enumerate(POOL_WINDOWS)]
        st["p_in"] = jnp.concatenate(pooled, axis=1).astype(jnp.bfloat16)

    def out_proj(s, st, k):
        if k == 0:
            st["mix"] = _dot(st["y_conv"], w_out_ref[0:conv_ch, :])
        else:
            st["mix"] = st["mix"] + _dot(st["p_in"], w_comb[...])

    def norm1(s, st):
        x = x_ref[0, pl.ds(s * sub, sub), :]
        x1 = _layer_norm(alpha * x + (1.0 + g1) * st["mix"], ln1_g_ref[...], ln1_b_ref[...])
        st["x1"] = x1
        st["h2"] = (x1 * (1.0 + sc2) + sh2).astype(jnp.bfloat16)

    def mlp_in(s, st, c):
        c0 = c * FF_CHUNK
        hid = _dot(st["h2"], w1_ref[:, c0:c0 + FF_CHUNK])
        st.setdefault("hid", []).append(jnp.square(jnp.maximum(hid, 0.0)).astype(jnp.bfloat16))

    def mlp_out(s, st, parts=1):
        hid = jnp.concatenate(st["hid"], axis=1)
        rows = sub // parts
        st["f"] = [_dot(hid[r:r + rows, :], w2_ref[...]) for r in range(0, sub, rows)]

    def norm2(s, st):
        rows = sub // len(st["f"])
        for i, f in enumerate(st["f"]):
            r = slice(i * rows, (i + 1) * rows)
            o_ref[0, pl.ds(s * sub + i * rows, rows), :] = _layer_norm(
                alpha * st["x1"][r] + (1.0 + g2) * f, ln2_g_ref[...], ln2_b_ref[...])

    n_in = w_in_ref.shape[1] // COL_PIECE
    n_ff = d_ff // FF_CHUNK

    assert n_sub == 2 and n_in in (2, 4)
    subs = [{} for _ in range(n_sub)]
    a, b = subs
    load(0, a)
    load(1, b)
    for n in range(n_in):
        in_proj_both(n)
    conv(0, a)
    pool(0, a)
    out_proj(0, a, 0)
    conv(1, b)
    out_proj(0, a, 1)
    pool(1, b)
    out_proj(1, b, 0)
    norm1(0, a)
    out_proj(1, b, 1)
    mlp_in(0, a, 0)
    norm1(1, b)
    for c in range(1, n_ff):
        mlp_in(0, a, c)
    mlp_out(0, a)
    mlp_in(1, b, 0)
    norm2(0, a)
    for c in range(1, n_ff):
        mlp_in(1, b, c)
    mlp_out(1, b, parts=LAST_OUT_PARTS)
    norm2(1, b)


def _const_spec(shape):
    return pl.BlockSpec(shape, lambda b, j: (0,) * len(shape), pipeline_mode=pl.Buffered(1))


def _layer(x, c, w_ada, b_ada, w_in, conv_w, w_pool, pool_scale, w_out, ln1_g, ln1_b, w1, w2,
           ln2_g, ln2_b, *, alpha, layer):
    batch, seq, d = x.shape
    conv_ch = conv_w.shape[1]
    n_groups, group_dim, _ = w_pool.shape
    pool_width = n_groups * group_dim
    tm, sub = TOKEN_TILE, SUB_TILE
    assert seq % tm == 0 and tm % sub == 0 and sub >= POOL_HALO and batch <= MOD_ROWS
    assert max(POOL_WINDOWS) <= POOL_HALO and CONV_K - 1 <= CONV_HALO
    assert all(win == 2 ** (g + 1) for g, win in enumerate(POOL_WINDOWS))
    assert w1.shape[2] % FF_CHUNK == 0 and STAGE_ROWS % group_dim == 0
    assert COL_PIECE % conv_ch == 0 and pool_width == conv_ch and w_in.shape[2] % COL_PIECE == 0
    streamed = (w_ada, w_in, w_out, w1, w2)
    assert all(w.shape[1] % STAGE_ROWS == 0 and w.shape[2] % STAGE_COLS == 0 for w in streamed)
    bf16 = jnp.bfloat16
    row = lambda a: a.reshape(1, -1)
    operands = (x, c, w_ada, row(b_ada), w_in, conv_w, w_pool, row(pool_scale), w_out,
                row(ln1_g), row(ln1_b), w1, w2, row(ln2_g), row(ln2_b))
    in_specs = [pl.BlockSpec((1, tm, d), lambda b, j: (b, j, 0))]
    in_specs += [pl.BlockSpec(memory_space=pl.ANY) if any(a is w for w in streamed)
                 else _const_spec(a.shape) for a in operands[1:]]
    return pl.pallas_call(
        functools.partial(_layer_kernel, alpha=alpha, sub=sub, layer=layer),
        grid=(batch, seq // tm),
        in_specs=in_specs,
        out_specs=pl.BlockSpec((1, tm, d), lambda b, j: (b, j, 0)),
        out_shape=jax.ShapeDtypeStruct(x.shape, x.dtype),
        scratch_shapes=[
            pltpu.VMEM((CONV_HALO, conv_ch), jnp.float32),
            pltpu.VMEM((POOL_HALO, pool_width), jnp.float32),
            pltpu.VMEM((pool_width, d), bf16),
            pltpu.VMEM((MOD_ROWS, w_ada.shape[2]), jnp.float32),
            pltpu.VMEM(w_in.shape[1:], bf16),
            pltpu.VMEM((conv_ch, d), bf16),
            pltpu.VMEM(w1.shape[1:], bf16),
            pltpu.VMEM(w2.shape[1:], bf16),
            pltpu.VMEM((STAGE_SLOTS, STAGE_ROWS, STAGE_COLS), jnp.float32),
            pltpu.SemaphoreType.DMA((STAGE_SLOTS,)),
        ],
        compiler_params=pltpu.CompilerParams(
            dimension_semantics=("arbitrary", "arbitrary"),
            vmem_limit_bytes=VMEM_LIMIT_BYTES),
        name="hybrid_layer",
    )(*operands)


def kernel(x, c, w_ada, b_ada, w_in, conv_w, w_pool, pool_scale, w_out, ln1_g, ln1_b, w_mlp_in,
           w_mlp_out, ln2_g, ln2_b):
    depth = w_in.shape[0]
    alpha = (2.0 * depth) ** 0.25
    for l in range(depth):
        x = _layer(x, c, w_ada, b_ada[l], w_in, conv_w[l], w_pool[l], pool_scale[l], w_out,
                   ln1_g[l], ln1_b[l], w_mlp_in, w_mlp_out, ln2_g[l], ln2_b[l], alpha=alpha,
                   layer=l)
    return x
```

```python
import functools

import jax
import jax.numpy as jnp
from jax import lax
from jax.experimental import pallas as pl
from jax.experimental.pallas import tpu as pltpu

CONV_K = 3
POOL_WINDOWS = (2, 4, 8, 16)
N_MOD = 6
LN_EPS = 1e-5

CONV_HALO = 8
POOL_HALO = 16
MOD_ROWS = 8

TOKEN_TILE = 1024
SUB_TILE = 512
COL_PIECE = 1024
FF_CHUNK = 1024
LAST_OUT_PARTS = 2
STAGE_SLOTS = 4
STAGE_ROWS = 512
STAGE_COLS = 1024
VMEM_LIMIT_BYTES = 60 * 1024 * 1024


def _layer_norm(r, g, b):
    mu = jnp.mean(r, axis=-1, keepdims=True)
    d = r - mu
    var = jnp.mean(d * d, axis=-1, keepdims=True)
    return d * lax.rsqrt(var + LN_EPS) * g + b


def _dot(a, b):
    return jnp.dot(a, b, preferred_element_type=jnp.float32)


def _layer_kernel(x_ref, c_ref, w_ada_hbm, b_ada_ref, w_in_hbm, conv_w_ref, w_pool_ref,
                  pool_scale_ref, w_out_hbm, ln1_g_ref, ln1_b_ref, w1_hbm, w2_hbm, ln2_g_ref,
                  ln2_b_ref, o_ref,
                  carry_u, carry_v, w_comb, mod_s, w_in_ref, w_out_ref, w1_ref, w2_ref, stage,
                  stage_sem, *, alpha, sub, layer):
    tm = x_ref.shape[1]
    d = x_ref.shape[2]
    n_sub = tm // sub
    conv_ch = conv_w_ref.shape[1]
    n_groups, group_dim, _ = w_pool_ref.shape
    d_ff = w1_ref.shape[1]
    j = pl.program_id(1)

    @pl.when(j == 0)
    def _():
        carry_u[...] = jnp.zeros(carry_u.shape, carry_u.dtype)
        carry_v[...] = jnp.zeros(carry_v.shape, carry_v.dtype)

    @pl.when((pl.program_id(0) == 0) & (j == 0))
    def _():
        c = c_ref[...]
        cond = c * jax.nn.sigmoid(c)
        cond = jnp.concatenate(
            [cond, jnp.zeros((MOD_ROWS - c.shape[0], d), cond.dtype)], axis=0).astype(jnp.bfloat16)

        def into_mod(r0, c0, block):
            part = _dot(cond[:, r0:r0 + STAGE_ROWS], block.astype(jnp.bfloat16))
            base = b_ada_ref[:, c0:c0 + STAGE_COLS] if r0 == 0 else mod_s[:, c0:c0 + STAGE_COLS]
            mod_s[:, c0:c0 + STAGE_COLS] = base + part

        def into(dst):
            def keep(r0, c0, block):
                dst[r0:r0 + STAGE_ROWS, c0:c0 + STAGE_COLS] = block.astype(dst.dtype)
            return keep

        def into_out_proj(r0, c0, block):
            if r0 < conv_ch:
                into(w_out_ref)(r0, c0, block)
            for g in range(n_groups):
                g0 = conv_ch + g * group_dim
                if r0 <= g0 and g0 + group_dim <= r0 + STAGE_ROWS:
                    scaled = w_pool_ref[g] * pool_scale_ref[:, g * group_dim:(g + 1) * group_dim]
                    w_comb[g * group_dim:(g + 1) * group_dim, c0:c0 + STAGE_COLS] = jnp.dot(
                        scaled, block[g0 - r0:g0 - r0 + group_dim, :],
                        precision=lax.Precision.HIGHEST,
                        preferred_element_type=jnp.float32).astype(w_comb.dtype)

        blocks = [(hbm, r0, c0, consume)
                  for hbm, shape, consume in (
                      (w_ada_hbm, (d, mod_s.shape[1]), into_mod),
                      (w_in_hbm, w_in_ref.shape, into(w_in_ref)),
                      (w_out_hbm, (d, d), into_out_proj),
                      (w1_hbm, w1_ref.shape, into(w1_ref)),
                      (w2_hbm, w2_ref.shape, into(w2_ref)))
                  for r0 in range(0, shape[0], STAGE_ROWS)
                  for c0 in range(0, shape[1], STAGE_COLS)]

        def fetch(i):
            hbm, r0, c0, _ = blocks[i]
            return pltpu.make_async_copy(
                hbm.at[layer, pl.ds(r0, STAGE_ROWS), pl.ds(c0, STAGE_COLS)],
                stage.at[i % STAGE_SLOTS], stage_sem.at[i % STAGE_SLOTS])

        ahead = STAGE_SLOTS - 1
        for i in range(min(ahead, len(blocks))):
            fetch(i).start()
        for i, (_, r0, c0, consume) in enumerate(blocks):
            if i + ahead < len(blocks):
                fetch(i + ahead).start()
            fetch(i).wait()
            consume(r0, c0, stage[i % STAGE_SLOTS])

    mod = mod_s[pl.ds(pl.program_id(0), 1), :]
    sh1, sc1, g1, sh2, sc2, g2 = (mod[:, i * d:(i + 1) * d] for i in range(N_MOD))
    cw = conv_w_ref[...]

    def load(s, st):
        x = x_ref[0, pl.ds(s * sub, sub), :]
        st["h"] = (x * (1.0 + sc1) + sh1).astype(jnp.bfloat16)
        st["z"] = {}

    def in_proj(s, st, n):
        piece = _dot(st["h"], w_in_ref[:, n * COL_PIECE:(n + 1) * COL_PIECE])
        for k in range(COL_PIECE // conv_ch):
            st["z"][n * (COL_PIECE // conv_ch) + k] = piece[:, k * conv_ch:(k + 1) * conv_ch]

    def with_history(s, st, key, cur, carry, halo):
        st[key] = cur[sub - halo:, :]
        if s == n_sub - 1:
            carry[...] = st[key]
        return jnp.concatenate([carry[...] if s == 0 else subs[s - 1][key], cur], axis=0)

    def conv(s, st):
        u = st["z"][1] * st["z"][2]
        ext = with_history(s, st, "u_tail", u, carry_u, CONV_HALO)
        acc = u * cw[CONV_K - 1:CONV_K]
        for k in range(CONV_K - 1):
            shifted = pltpu.roll(ext, CONV_K - 1 - k, axis=0)[CONV_HALO:, :]
            acc = acc + shifted * cw[k:k + 1]
        st["y_conv"] = (st["z"][0] * acc).astype(jnp.bfloat16)

    def pool(s, st):
        v_pool = st["z"][3]
        ext = with_history(s, st, "v_tail", v_pool, carry_v, POOL_HALO)
        sums, part = [], ext
        for g, win in enumerate(POOL_WINDOWS):
            part = part + pltpu.roll(part, win // 2, axis=0)
            sums.append(part[POOL_HALO:, 0:group_dim])
            part = part[:, group_dim:]
        pos = (j * tm + s * sub + 1
               + lax.broadcasted_iota(jnp.int32, (sub, 1), 0)).astype(jnp.float32)
        pooled = [
            sums[g] / jnp.minimum(pos, float(win)) - v_pool[:, g * group_dim:(g + 1) * group_dim]
            for g, win in enumerate(POOL_WINDOWS)]
        st["p_in"] = jnp.concatenate(pooled, axis=1).astype(jnp.bfloat16)

    def out_proj(s, st, k):
        if k == 0:
            st["mix"] = _dot(st["y_conv"], w_out_ref[0:conv_ch, :])
        else:
            st["mix"] = st["mix"] + _dot(st["p_in"], w_comb[...])

    def norm1(s, st):
        x = x_ref[0, pl.ds(s * sub, sub), :]
        x1 = _layer_norm(alpha * x + (1.0 + g1) * st["mix"], ln1_g_ref[...], ln1_b_ref[...])
        st["x1"] = x1
        st["h2"] = (x1 * (1.0 + sc2) + sh2).astype(jnp.bfloat16)

    def mlp_in(s, st, c):
        c0 = c * FF_CHUNK
        hid = _dot(st["h2"], w1_ref[:, c0:c0 + FF_CHUNK])
        st.setdefault("hid", []).append(jnp.square(jnp.maximum(hid, 0.0)).astype(jnp.bfloat16))

    def mlp_out(s, st, parts=1):
        hid = jnp.concatenate(st["hid"], axis=1)
        rows = sub // parts
        st["f"] = [_dot(hid[r:r + rows, :], w2_ref[...]) for r in range(0, sub, rows)]

    def norm2(s, st):
        rows = sub // len(st["f"])
        for i, f in enumerate(st["f"]):
            r = slice(i * rows, (i + 1) * rows)
            o_ref[0, pl.ds(s * sub + i * rows, rows), :] = _layer_norm(
                alpha * st["x1"][r] + (1.0 + g2) * f, ln2_g_ref[...], ln2_b_ref[...])

    n_in = w_in_ref.shape[1] // COL_PIECE
    n_ff = d_ff // FF_CHUNK

    assert n_sub == 2 and n_in in (2, 4)
    subs = [{} for _ in range(n_sub)]
    a, b = subs
    load(0, a)
    for n in range(n_in):
        in_proj(0, a, n)
    load(1, b)
    in_proj(1, b, 0)
    conv(0, a)
    if n_in == 4:
        in_proj(1, b, 1)
    pool(0, a)
    for n in range(n_in // 2, n_in):
        in_proj(1, b, n)
    out_proj(0, a, 0)
    conv(1, b)
    out_proj(0, a, 1)
    pool(1, b)
    out_proj(1, b, 0)
    norm1(0, a)
    out_proj(1, b, 1)
    mlp_in(0, a, 0)
    norm1(1, b)
    for c in range(1, n_ff):
        mlp_in(0, a, c)
    mlp_out(0, a)
    mlp_in(1, b, 0)
    norm2(0, a)
    for c in range(1, n_ff):
        mlp_in(1, b, c)
    mlp_out(1, b, parts=LAST_OUT_PARTS)
    norm2(1, b)


def _const_spec(shape):
    return pl.BlockSpec(shape, lambda b, j: (0,) * len(shape), pipeline_mode=pl.Buffered(1))


def _layer(x, c, w_ada, b_ada, w_in, conv_w, w_pool, pool_scale, w_out, ln1_g, ln1_b, w1, w2,
           ln2_g, ln2_b, *, alpha, layer):
    batch, seq, d = x.shape
    conv_ch = conv_w.shape[1]
    n_groups, group_dim, _ = w_pool.shape
    pool_width = n_groups * group_dim
    tm, sub = TOKEN_TILE, SUB_TILE
    assert seq % tm == 0 and tm % sub == 0 and sub >= POOL_HALO and batch <= MOD_ROWS
    assert max(POOL_WINDOWS) <= POOL_HALO and CONV_K - 1 <= CONV_HALO
    assert all(win == 2 ** (g + 1) for g, win in enumerate(POOL_WINDOWS))
    assert w1.shape[2] % FF_CHUNK == 0 and STAGE_ROWS % group_dim == 0
    assert COL_PIECE % conv_ch == 0 and pool_width == conv_ch and w_in.shape[2] % COL_PIECE == 0
    streamed = (w_ada, w_in, w_out, w1, w2)
    assert all(w.shape[1] % STAGE_ROWS == 0 and w.shape[2] % STAGE_COLS == 0 for w in streamed)
    bf16 = jnp.bfloat16
    row = lambda a: a.reshape(1, -1)
    operands = (x, c, w_ada, row(b_ada), w_in, conv_w, w_pool, row(pool_scale), w_out,
                row(ln1_g), row(ln1_b), w1, w2, row(ln2_g), row(ln2_b))
    in_specs = [pl.BlockSpec((1, tm, d), lambda b, j: (b, j, 0))]
    in_specs += [pl.BlockSpec(memory_space=pl.ANY) if any(a is w for w in streamed)
                 else _const_spec(a.shape) for a in operands[1:]]
    return pl.pallas_call(
        functools.partial(_layer_kernel, alpha=alpha, sub=sub, layer=layer),
        grid=(batch, seq // tm),
        in_specs=in_specs,
        out_specs=pl.BlockSpec((1, tm, d), lambda b, j: (b, j, 0)),
        out_shape=jax.ShapeDtypeStruct(x.shape, x.dtype),
        scratch_shapes=[
            pltpu.VMEM((CONV_HALO, conv_ch), jnp.float32),
            pltpu.VMEM((POOL_HALO, pool_width), jnp.float32),
            pltpu.VMEM((pool_width, d), bf16),
            pltpu.VMEM((MOD_ROWS, w_ada.shape[2]), jnp.float32),
            pltpu.VMEM(w_in.shape[1:], bf16),
            pltpu.VMEM((conv_ch, d), bf16),
            pltpu.VMEM(w1.shape[1:], bf16),
            pltpu.VMEM(w2.shape[1:], bf16),
            pltpu.VMEM((STAGE_SLOTS, STAGE_ROWS, STAGE_COLS), jnp.float32),
            pltpu.SemaphoreType.DMA((STAGE_SLOTS,)),
        ],
        compiler_params=pltpu.CompilerParams(
            dimension_semantics=("arbitrary", "arbitrary"),
            vmem_limit_bytes=VMEM_LIMIT_BYTES),
        name="hybrid_layer",
    )(*operands)


def kernel(x, c, w_ada, b_ada, w_in, conv_w, w_pool, pool_scale, w_out, ln1_g, ln1_b, w_mlp_in,
           w_mlp_out, ln2_g, ln2_b):
    depth = w_in.shape[0]
    alpha = (2.0 * depth) ** 0.25
    for l in range(depth):
        x = _layer(x, c, w_ada, b_ada[l], w_in, conv_w[l], w_pool[l], pool_scale[l], w_out,
                   ln1_g[l], ln1_b[l], w_mlp_in, w_mlp_out, ln2_g[l], ln2_b[l], alpha=alpha,
                   layer=l)
    return x
```

```python
import functools

import jax
import jax.numpy as jnp
from jax import lax
from jax.experimental import pallas as pl
from jax.experimental.pallas import tpu as pltpu

CONV_K = 3
POOL_WINDOWS = (2, 4, 8, 16)
N_MOD = 6
LN_EPS = 1e-5

CONV_HALO = 8
POOL_HALO = 16
MOD_ROWS = 8

TOKEN_TILE = 1024
SUB_TILE = 512
COL_PIECE = 1024
FF_CHUNK = 1024
LAST_OUT_PARTS = 2
OUT_SLOTS = 2
STAGE_SLOTS = 3
STAGE_ROWS = 512
STAGE_COLS = 1024
VMEM_LIMIT_BYTES = 60 * 1024 * 1024


def _layer_norm(r, g, b):
    mu = jnp.mean(r, axis=-1, keepdims=True)
    d = r - mu
    var = jnp.mean(d * d, axis=-1, keepdims=True)
    return d * lax.rsqrt(var + LN_EPS) * g + b


def _dot(a, b):
    return jnp.dot(a, b, preferred_element_type=jnp.float32)


def _layer_kernel(x_ref, c_ref, w_ada_hbm, b_ada_ref, w_in_hbm, conv_w_ref, w_pool_ref,
                  pool_scale_ref, w_out_hbm, ln1_g_ref, ln1_b_ref, w1_hbm, w2_hbm, ln2_g_ref,
                  ln2_b_ref, o_hbm,
                  carry_u, carry_v, w_comb, mod_s, w_in_ref, w_out_ref, w1_ref, w2_ref, stage,
                  stage_sem, out_buf, out_sem, late_f, late_x1, *, alpha, sub, layer):
    tm = x_ref.shape[1]
    d = x_ref.shape[2]
    n_sub = tm // sub
    conv_ch = conv_w_ref.shape[1]
    n_groups, group_dim, _ = w_pool_ref.shape
    d_ff = w1_ref.shape[1]
    j = pl.program_id(1)
    n_steps = pl.num_programs(0) * pl.num_programs(1)
    step = pl.program_id(0) * pl.num_programs(1) + j
    slot = step % OUT_SLOTS
    late = late_f.shape[0]

    def write_back(slot_, buf_row0, rows, hbm_row0):
        return pltpu.make_async_copy(out_buf.at[slot_, pl.ds(buf_row0, rows), :],
                                     o_hbm.at[pl.ds(hbm_row0, rows), :], out_sem.at[slot_])

    @pl.when(step == OUT_SLOTS)
    def _():
        write_back(slot, late, tm - late, 0).wait()

    @pl.when(step > OUT_SLOTS)
    def _():
        write_back(slot, 0, tm, 0).wait()

    @pl.when(j == 0)
    def _():
        carry_u[...] = jnp.zeros(carry_u.shape, carry_u.dtype)
        carry_v[...] = jnp.zeros(carry_v.shape, carry_v.dtype)

    @pl.when((pl.program_id(0) == 0) & (j == 0))
    def _():
        late_f[...] = jnp.zeros(late_f.shape, late_f.dtype)
        late_x1[...] = jnp.zeros(late_x1.shape, late_x1.dtype)
        c = c_ref[...]
        cond = c * jax.nn.sigmoid(c)
        cond = jnp.concatenate(
            [cond, jnp.zeros((MOD_ROWS - c.shape[0], d), cond.dtype)], axis=0).astype(jnp.bfloat16)

        def into_mod(r0, c0, block):
            part = _dot(cond[:, r0:r0 + STAGE_ROWS], block.astype(jnp.bfloat16))
            base = b_ada_ref[:, c0:c0 + STAGE_COLS] if r0 == 0 else mod_s[:, c0:c0 + STAGE_COLS]
            mod_s[:, c0:c0 + STAGE_COLS] = base + part

        def into(dst):
            def keep(r0, c0, block):
                dst[r0:r0 + STAGE_ROWS, c0:c0 + STAGE_COLS] = block.astype(dst.dtype)
            return keep

        def into_out_proj(r0, c0, block):
            if r0 < conv_ch:
                into(w_out_ref)(r0, c0, block)
            for g in range(n_groups):
                g0 = conv_ch + g * group_dim
                if r0 <= g0 and g0 + group_dim <= r0 + STAGE_ROWS:
                    scaled = w_pool_ref[g] * pool_scale_ref[:, g * group_dim:(g + 1) * group_dim]
                    w_comb[g * group_dim:(g + 1) * group_dim, c0:c0 + STAGE_COLS] = jnp.dot(
                        scaled, block[g0 - r0:g0 - r0 + group_dim, :],
                        precision=lax.Precision.HIGHEST,
                        preferred_element_type=jnp.float32).astype(w_comb.dtype)

        blocks = [(hbm, r0, c0, consume)
                  for hbm, shape, consume in (
                      (w_ada_hbm, (d, mod_s.shape[1]), into_mod),
                      (w_in_hbm, w_in_ref.shape, into(w_in_ref)),
                      (w_out_hbm, (d, d), into_out_proj),
                      (w1_hbm, w1_ref.shape, into(w1_ref)),
                      (w2_hbm, w2_ref.shape, into(w2_ref)))
                  for r0 in range(0, shape[0], STAGE_ROWS)
                  for c0 in range(0, shape[1], STAGE_COLS)]

        def fetch(i):
            hbm, r0, c0, _ = blocks[i]
            return pltpu.make_async_copy(
                hbm.at[layer, pl.ds(r0, STAGE_ROWS), pl.ds(c0, STAGE_COLS)],
                stage.at[i % STAGE_SLOTS], stage_sem.at[i % STAGE_SLOTS])

        ahead = STAGE_SLOTS - 1
        for i in range(min(ahead, len(blocks))):
            fetch(i).start()
        for i, (_, r0, c0, consume) in enumerate(blocks):
            if i + ahead < len(blocks):
                fetch(i + ahead).start()
            fetch(i).wait()
            consume(r0, c0, stage[i % STAGE_SLOTS])

    mod = mod_s[pl.ds(pl.program_id(0), 1), :]
    sh1, sc1, g1, sh2, sc2, g2 = (mod[:, i * d:(i + 1) * d] for i in range(N_MOD))
    cw = conv_w_ref[...]

    def load(s, st):
        x = x_ref[0, pl.ds(s * sub, sub), :]
        st["h"] = (x * (1.0 + sc1) + sh1).astype(jnp.bfloat16)
        st["z"] = {}

    def in_proj(s, st, n):
        piece = _dot(st["h"], w_in_ref[:, n * COL_PIECE:(n + 1) * COL_PIECE])
        for k in range(COL_PIECE // conv_ch):
            st["z"][n * (COL_PIECE // conv_ch) + k] = piece[:, k * conv_ch:(k + 1) * conv_ch]

    def with_history(s, st, key, cur, carry, halo):
        st[key] = cur[sub - halo:, :]
        if s == n_sub - 1:
            carry[...] = st[key]
        return jnp.concatenate([carry[...] if s == 0 else subs[s - 1][key], cur], axis=0)

    def conv(s, st):
        u = st["z"][1] * st["z"][2]
        ext = with_history(s, st, "u_tail", u, carry_u, CONV_HALO)
        acc = u * cw[CONV_K - 1:CONV_K]
        for k in range(CONV_K - 1):
            shifted = pltpu.roll(ext, CONV_K - 1 - k, axis=0)[CONV_HALO:, :]
            acc = acc + shifted * cw[k:k + 1]
        st["y_conv"] = (st["z"][0] * acc).astype(jnp.bfloat16)

    def pool(s, st):
        v_pool = st["z"][3]
        ext = with_history(s, st, "v_tail", v_pool, carry_v, POOL_HALO)
        sums, part = [], ext
        for g, win in enumerate(POOL_WINDOWS):
            part = part + pltpu.roll(part, win // 2, axis=0)
            sums.append(part[POOL_HALO:, 0:group_dim])
            part = part[:, group_dim:]
        pos = (j * tm + s * sub + 1
               + lax.broadcasted_iota(jnp.int32, (sub, 1), 0)).astype(jnp.float32)
        pooled = [
            sums[g] / jnp.minimum(pos, float(win)) - v_pool[:, g * group_dim:(g + 1) * group_dim]
            for g, win in enumerate(POOL_WINDOWS)]
        st["p_in"] = jnp.concatenate(pooled, axis=1).astype(jnp.bfloat16)

    def out_proj(s, st, k):
        if k == 0:
            st["mix"] = _dot(st["y_conv"], w_out_ref[0:conv_ch, :])
        else:
            st["mix"] = st["mix"] + _dot(st["p_in"], w_comb[...])

    def norm1(s, st):
        x = x_ref[0, pl.ds(s * sub, sub), :]
        x1 = _layer_norm(alpha * x + (1.0 + g1) * st["mix"], ln1_g_ref[...], ln1_b_ref[...])
        st["x1"] = x1
        st["h2"] = (x1 * (1.0 + sc2) + sh2).astype(jnp.bfloat16)

    def mlp_in(s, st, c):
        c0 = c * FF_CHUNK
        hid = _dot(st["h2"], w1_ref[:, c0:c0 + FF_CHUNK])
        st.setdefault("hid", []).append(jnp.square(jnp.maximum(hid, 0.0)).astype(jnp.bfloat16))

    def mlp_out(s, st, parts=1):
        hid = jnp.concatenate(st["hid"], axis=1)
        rows = sub // parts
        st["f"] = [_dot(hid[r:r + rows, :], w2_ref[...]) for r in range(0, sub, rows)]

    def finish(x1, f, gate, slot_, buf_row0):
        out_buf[slot_, pl.ds(buf_row0, f.shape[0]), :] = _layer_norm(
            alpha * x1 + (1.0 + gate) * f, ln2_g_ref[...], ln2_b_ref[...])

    def late_norm2(slot_, owner_step):
        gate = mod_s[pl.ds(owner_step // pl.num_programs(1), 1), (N_MOD - 1) * d:]
        finish(late_x1[...], late_f[...], gate, slot_, 0)

    def norm2(s, st, defer_last=False):
        rows = sub // len(st["f"])
        for i, f in enumerate(st["f"]):
            x1 = st["x1"][i * rows:(i + 1) * rows]
            if defer_last and i == len(st["f"]) - 1:
                late_f[...] = f
                late_x1[...] = x1
            else:
                finish(x1, f, g2, slot, late + s * sub + i * rows)

    n_in = w_in_ref.shape[1] // COL_PIECE
    n_ff = d_ff // FF_CHUNK

    assert n_sub == 2 and n_in in (2, 4)
    subs = [{} for _ in range(n_sub)]
    a, b = subs
    load(0, a)
    late_norm2(slot, jnp.maximum(step - 1, 0))
    for n in range(n_in):
        in_proj(0, a, n)
    load(1, b)
    in_proj(1, b, 0)
    conv(0, a)
    if n_in == 4:
        in_proj(1, b, 1)
    pool(0, a)
    for n in range(n_in // 2, n_in):
        in_proj(1, b, n)
    out_proj(0, a, 0)
    conv(1, b)
    out_proj(0, a, 1)
    pool(1, b)
    out_proj(1, b, 0)
    norm1(0, a)
    out_proj(1, b, 1)
    mlp_in(0, a, 0)
    norm1(1, b)
    for c in range(1, n_ff):
        mlp_in(0, a, c)
    mlp_out(0, a)
    mlp_in(1, b, 0)
    norm2(0, a)
    for c in range(1, n_ff):
        mlp_in(1, b, c)
    mlp_out(1, b, parts=LAST_OUT_PARTS)
    norm2(1, b, defer_last=True)

    @pl.when(step == 0)
    def _():
        write_back(slot, late, tm - late, 0).start()

    @pl.when(step > 0)
    def _():
        write_back(slot, 0, tm, step * tm - late).start()

    @pl.when(step == n_steps - 1)
    def _():
        other = (step + 1) % OUT_SLOTS
        write_back(other, 0, tm, 0).wait()
        late_norm2(other, step)
        tail = write_back(other, 0, late, n_steps * tm - late)
        tail.start()
        tail.wait()
        write_back(slot, 0, tm, 0).wait()


def _const_spec(shape):
    return pl.BlockSpec(shape, lambda b, j: (0,) * len(shape), pipeline_mode=pl.Buffered(1))


def _layer(x, c, w_ada, b_ada, w_in, conv_w, w_pool, pool_scale, w_out, ln1_g, ln1_b, w1, w2,
           ln2_g, ln2_b, *, alpha, layer):
    batch, seq, d = x.shape
    conv_ch = conv_w.shape[1]
    n_groups, group_dim, _ = w_pool.shape
    pool_width = n_groups * group_dim
    tm, sub = TOKEN_TILE, SUB_TILE
    assert seq % tm == 0 and tm % sub == 0 and sub >= POOL_HALO and batch <= MOD_ROWS
    assert batch * seq // tm > OUT_SLOTS
    assert max(POOL_WINDOWS) <= POOL_HALO and CONV_K - 1 <= CONV_HALO
    assert all(win == 2 ** (g + 1) for g, win in enumerate(POOL_WINDOWS))
    assert w1.shape[2] % FF_CHUNK == 0 and STAGE_ROWS % group_dim == 0
    assert COL_PIECE % conv_ch == 0 and pool_width == conv_ch and w_in.shape[2] % COL_PIECE == 0
    streamed = (w_ada, w_in, w_out, w1, w2)
    assert all(w.shape[1] % STAGE_ROWS == 0 and w.shape[2] % STAGE_COLS == 0 for w in streamed)
    bf16 = jnp.bfloat16
    row = lambda a: a.reshape(1, -1)
    operands = (x, c, w_ada, row(b_ada), w_in, conv_w, w_pool, row(pool_scale), w_out,
                row(ln1_g), row(ln1_b), w1, w2, row(ln2_g), row(ln2_b))
    in_specs = [pl.BlockSpec((1, tm, d), lambda b, j: (b, j, 0))]
    in_specs += [pl.BlockSpec(memory_space=pl.ANY) if any(a is w for w in streamed)
                 else _const_spec(a.shape) for a in operands[1:]]
    return pl.pallas_call(
        functools.partial(_layer_kernel, alpha=alpha, sub=sub, layer=layer),
        grid=(batch, seq // tm),
        in_specs=in_specs,
        out_specs=pl.BlockSpec(memory_space=pl.ANY),
        out_shape=jax.ShapeDtypeStruct((batch * seq, d), x.dtype),
        scratch_shapes=[
            pltpu.VMEM((CONV_HALO, conv_ch), jnp.float32),
            pltpu.VMEM((POOL_HALO, pool_width), jnp.float32),
            pltpu.VMEM((pool_width, d), bf16),
            pltpu.VMEM((MOD_ROWS, w_ada.shape[2]), jnp.float32),
            pltpu.VMEM(w_in.shape[1:], bf16),
            pltpu.VMEM((conv_ch, d), bf16),
            pltpu.VMEM(w1.shape[1:], bf16),
            pltpu.VMEM(w2.shape[1:], bf16),
            pltpu.VMEM((STAGE_SLOTS, STAGE_ROWS, STAGE_COLS), jnp.float32),
            pltpu.SemaphoreType.DMA((STAGE_SLOTS,)),
            pltpu.VMEM((OUT_SLOTS, tm, d), jnp.float32),
            pltpu.SemaphoreType.DMA((OUT_SLOTS,)),
            pltpu.VMEM((sub // LAST_OUT_PARTS, d), jnp.float32),
            pltpu.VMEM((sub // LAST_OUT_PARTS, d), jnp.float32),
        ],
        compiler_params=pltpu.CompilerParams(
            dimension_semantics=("arbitrary", "arbitrary"),
            vmem_limit_bytes=VMEM_LIMIT_BYTES),
        name="hybrid_layer",
    )(*operands).reshape(x.shape)


def kernel(x, c, w_ada, b_ada, w_in, conv_w, w_pool, pool_scale, w_out, ln1_g, ln1_b, w_mlp_in,
           w_mlp_out, ln2_g, ln2_b):
    depth = w_in.shape[0]
    alpha = (2.0 * depth) ** 0.25
    for l in range(depth):
        x = _layer(x, c, w_ada, b_ada[l], w_in, conv_w[l], w_pool[l], pool_scale[l], w_out,
                   ln1_g[l], ln1_b[l], w_mlp_in, w_mlp_out, ln2_g[l], ln2_b[l], alpha=alpha,
                   layer=l)
    return x
```

```python
import functools

import jax
import jax.numpy as jnp
from jax import lax
from jax.experimental import pallas as pl
from jax.experimental.pallas import tpu as pltpu

CONV_K = 3
POOL_WINDOWS = (2, 4, 8, 16)
N_MOD = 6
LN_EPS = 1e-5

CONV_HALO = 8
POOL_HALO = 16
MOD_ROWS = 8

TOKEN_TILE = 1024
SUB_TILE = 512
COL_PIECE = 1024
FF_CHUNK = 1024
LAST_OUT_PARTS = 2
IO_SLOTS = 2
STAGE_SLOTS = 3
STAGE_ROWS = 512
STAGE_COLS = 1024
VMEM_LIMIT_BYTES = 60 * 1024 * 1024


def _layer_norm(r, g, b):
    mu = jnp.mean(r, axis=-1, keepdims=True)
    d = r - mu
    var = jnp.mean(d * d, axis=-1, keepdims=True)
    return d * lax.rsqrt(var + LN_EPS) * g + b


def _dot(a, b):
    return jnp.dot(a, b, preferred_element_type=jnp.float32)


def _layer_kernel(x_hbm, c_ref, w_ada_hbm, b_ada_ref, w_in_hbm, conv_w_ref, w_pool_ref,
                  pool_scale_ref, w_out_hbm, ln1_g_ref, ln1_b_ref, w1_hbm, w2_hbm, ln2_g_ref,
                  ln2_b_ref, o_hbm,
                  carry_u, carry_v, w_comb, mod_s, w_in_ref, w_out_ref, w1_ref, w2_ref, stage,
                  stage_sem, x_buf, x_sem, out_buf, out_sem, late_f, late_x1,
                  *, alpha, sub, layer, n_tiles, tiles_per_seq):
    tm, d = x_buf.shape[1], x_buf.shape[2]
    n_sub = tm // sub
    conv_ch = conv_w_ref.shape[1]
    n_groups, group_dim, _ = w_pool_ref.shape
    d_ff = w1_ref.shape[1]
    late = late_f.shape[0]
    n_in = w_in_ref.shape[1] // COL_PIECE
    n_ff = d_ff // FF_CHUNK

    def x_fetch(t):
        return pltpu.make_async_copy(x_hbm.at[pl.ds(t * tm, tm), :], x_buf.at[t % IO_SLOTS],
                                     x_sem.at[t % IO_SLOTS])

    def write_back(slot, buf_row0, rows, hbm_row0):
        return pltpu.make_async_copy(out_buf.at[slot, pl.ds(buf_row0, rows), :],
                                     o_hbm.at[pl.ds(hbm_row0, rows), :], out_sem.at[slot])

    x_fetch(0).start()
    late_f[...] = jnp.zeros(late_f.shape, late_f.dtype)
    late_x1[...] = jnp.zeros(late_x1.shape, late_x1.dtype)

    c = c_ref[...]
    cond = c * jax.nn.sigmoid(c)
    cond = jnp.concatenate(
        [cond, jnp.zeros((MOD_ROWS - c.shape[0], d), cond.dtype)], axis=0).astype(jnp.bfloat16)

    def into_mod(r0, c0, block):
        part = _dot(cond[:, r0:r0 + STAGE_ROWS], block.astype(jnp.bfloat16))
        base = b_ada_ref[:, c0:c0 + STAGE_COLS] if r0 == 0 else mod_s[:, c0:c0 + STAGE_COLS]
        mod_s[:, c0:c0 + STAGE_COLS] = base + part

    def into(dst):
        def keep(r0, c0, block):
            dst[r0:r0 + STAGE_ROWS, c0:c0 + STAGE_COLS] = block.astype(dst.dtype)
        return keep

    def into_out_proj(r0, c0, block):
        if r0 < conv_ch:
            into(w_out_ref)(r0, c0, block)
        for g in range(n_groups):
            g0 = conv_ch + g * group_dim
            if r0 <= g0 and g0 + group_dim <= r0 + STAGE_ROWS:
                scaled = w_pool_ref[g] * pool_scale_ref[:, g * group_dim:(g + 1) * group_dim]
                w_comb[g * group_dim:(g + 1) * group_dim, c0:c0 + STAGE_COLS] = jnp.dot(
                    scaled, block[g0 - r0:g0 - r0 + group_dim, :],
                    precision=lax.Precision.HIGHEST,
                    preferred_element_type=jnp.float32).astype(w_comb.dtype)

    blocks = [(hbm, r0, c0, consume)
              for hbm, shape, consume in (
                  (w_ada_hbm, (d, mod_s.shape[1]), into_mod),
                  (w_in_hbm, w_in_ref.shape, into(w_in_ref)),
                  (w_out_hbm, (d, d), into_out_proj),
                  (w1_hbm, w1_ref.shape, into(w1_ref)),
                  (w2_hbm, w2_ref.shape, into(w2_ref)))
              for r0 in range(0, shape[0], STAGE_ROWS)
              for c0 in range(0, shape[1], STAGE_COLS)]

    def fetch(i):
        hbm, r0, c0, _ = blocks[i]
        return pltpu.make_async_copy(
            hbm.at[layer, pl.ds(r0, STAGE_ROWS), pl.ds(c0, STAGE_COLS)],
            stage.at[i % STAGE_SLOTS], stage_sem.at[i % STAGE_SLOTS])

    ahead = STAGE_SLOTS - 1
    for i in range(min(ahead, len(blocks))):
        fetch(i).start()
    for i, (_, r0, c0, consume) in enumerate(blocks):
        if i + ahead < len(blocks):
            fetch(i + ahead).start()
        fetch(i).wait()
        consume(r0, c0, stage[i % STAGE_SLOTS])

    cw = conv_w_ref[...]

    def finish(x1, f, gate, slot, buf_row0):
        out_buf[slot, pl.ds(buf_row0, f.shape[0]), :] = _layer_norm(
            alpha * x1 + (1.0 + gate) * f, ln2_g_ref[...], ln2_b_ref[...])

    def late_norm2(slot, owner_tile):
        gate = mod_s[pl.ds(owner_tile // tiles_per_seq, 1), (N_MOD - 1) * d:]
        finish(late_x1[...], late_f[...], gate, slot, 0)

    def tile_body(t, _):
        slot = t % IO_SLOTS
        j = t % tiles_per_seq

        x_fetch(t).wait()

        @pl.when(t + 1 < n_tiles)
        def _():
            x_fetch(t + 1).start()

        @pl.when(t == IO_SLOTS)
        def _():
            write_back(slot, late, tm - late, 0).wait()

        @pl.when(t > IO_SLOTS)
        def _():
            write_back(slot, 0, tm, 0).wait()

        @pl.when(j == 0)
        def _():
            carry_u[...] = jnp.zeros(carry_u.shape, carry_u.dtype)
            carry_v[...] = jnp.zeros(carry_v.shape, carry_v.dtype)

        mod = mod_s[pl.ds(t // tiles_per_seq, 1), :]
        sh1, sc1, g1, sh2, sc2, g2 = (mod[:, i * d:(i + 1) * d] for i in range(N_MOD))

        def load(s, st):
            x = x_buf[slot, pl.ds(s * sub, sub), :]
            st["h"] = (x * (1.0 + sc1) + sh1).astype(jnp.bfloat16)
            st["z"] = {}

        def in_proj(s, st, n):
            piece = _dot(st["h"], w_in_ref[:, n * COL_PIECE:(n + 1) * COL_PIECE])
            for k in range(COL_PIECE // conv_ch):
                st["z"][n * (COL_PIECE // conv_ch) + k] = piece[:, k * conv_ch:(k + 1) * conv_ch]

        def with_history(s, st, key, cur, carry, halo):
            st[key] = cur[sub - halo:, :]
            if s == n_sub - 1:
                carry[...] = st[key]
            return jnp.concatenate([carry[...] if s == 0 else subs[s - 1][key], cur], axis=0)

        def conv(s, st):
            u = st["z"][1] * st["z"][2]
            ext = with_history(s, st, "u_tail", u, carry_u, CONV_HALO)
            acc = u * cw[CONV_K - 1:CONV_K]
            for k in range(CONV_K - 1):
                shifted = pltpu.roll(ext, CONV_K - 1 - k, axis=0)[CONV_HALO:, :]
                acc = acc + shifted * cw[k:k + 1]
            st["y_conv"] = (st["z"][0] * acc).astype(jnp.bfloat16)

        def pool(s, st):
            v_pool = st["z"][3]
            ext = with_history(s, st, "v_tail", v_pool, carry_v, POOL_HALO)
            sums, part = [], ext
            for g, win in enumerate(POOL_WINDOWS):
                part = part + pltpu.roll(part, win // 2, axis=0)
                sums.append(part[POOL_HALO:, 0:group_dim])
                part = part[:, group_dim:]
            pos = (j * tm + s * sub + 1
                   + lax.broadcasted_iota(jnp.int32, (sub, 1), 0)).astype(jnp.float32)
            pooled = [
                sums[g] / jnp.minimum(pos, float(win))
                - v_pool[:, g * group_dim:(g + 1) * group_dim]
                for g, win in enumerate(POOL_WINDOWS)]
            st["p_in"] = jnp.concatenate(pooled, axis=1).astype(jnp.bfloat16)

        def out_proj(s, st, k):
            if k == 0:
                st["mix"] = _dot(st["y_conv"], w_out_ref[0:conv_ch, :])
            else:
                st["mix"] = st["mix"] + _dot(st["p_in"], w_comb[...])

        def norm1(s, st):
            x = x_buf[slot, pl.ds(s * sub, sub), :]
            x1 = _layer_norm(alpha * x + (1.0 + g1) * st["mix"], ln1_g_ref[...], ln1_b_ref[...])
            st["x1"] = x1
            st["h2"] = (x1 * (1.0 + sc2) + sh2).astype(jnp.bfloat16)

        def mlp_in(s, st, c):
            c0 = c * FF_CHUNK
            hid = _dot(st["h2"], w1_ref[:, c0:c0 + FF_CHUNK])
            st.setdefault("hid", []).append(
                jnp.square(jnp.maximum(hid, 0.0)).astype(jnp.bfloat16))

        def mlp_out(s, st, parts=1):
            hid = jnp.concatenate(st["hid"], axis=1)
            rows = sub // parts
            st["f"] = [_dot(hid[r:r + rows, :], w2_ref[...]) for r in range(0, sub, rows)]

        def norm2(s, st, defer_last=False):
            rows = sub // len(st["f"])
            for i, f in enumerate(st["f"]):
                x1 = st["x1"][i * rows:(i + 1) * rows]
                if defer_last and i == len(st["f"]) - 1:
                    late_f[...] = f
                    late_x1[...] = x1
                else:
                    finish(x1, f, g2, slot, late + s * sub + i * rows)

        assert n_sub == 2 and n_in in (2, 4)
        subs = [{} for _ in range(n_sub)]
        a, b = subs
        load(0, a)
        late_norm2(slot, jnp.maximum(t - 1, 0))
        for n in range(n_in):
            in_proj(0, a, n)
        load(1, b)
        in_proj(1, b, 0)
        conv(0, a)
        if n_in == 4:
            in_proj(1, b, 1)
        pool(0, a)
        for n in range(n_in // 2, n_in):
            in_proj(1, b, n)
        out_proj(0, a, 0)
        conv(1, b)
        out_proj(0, a, 1)
        pool(1, b)
        out_proj(1, b, 0)
        norm1(0, a)
        out_proj(1, b, 1)
        mlp_in(0, a, 0)
        norm1(1, b)
        for c in range(1, n_ff):
            mlp_in(0, a, c)
        mlp_out(0, a)
        mlp_in(1, b, 0)
        norm2(0, a)
        for c in range(1, n_ff):
            mlp_in(1, b, c)
        mlp_out(1, b, parts=LAST_OUT_PARTS)
        norm2(1, b, defer_last=True)

        @pl.when(t == 0)
        def _():
            write_back(slot, late, tm - late, 0).start()

        @pl.when(t > 0)
        def _():
            write_back(slot, 0, tm, t * tm - late).start()

        return 0

    lax.fori_loop(0, n_tiles, tile_body, 0)

    last = n_tiles - 1
    other = (last + 1) % IO_SLOTS
    write_back(other, 0, tm, 0).wait()
    late_norm2(other, last)
    tail = write_back(other, 0, late, n_tiles * tm - late)
    tail.start()
    tail.wait()
    write_back(last % IO_SLOTS, 0, tm, 0).wait()


def _layer(x, c, w_ada, b_ada, w_in, conv_w, w_pool, pool_scale, w_out, ln1_g, ln1_b, w1, w2,
           ln2_g, ln2_b, *, alpha, layer):
    batch, seq, d = x.shape
    conv_ch = conv_w.shape[1]
    n_groups, group_dim, _ = w_pool.shape
    pool_width = n_groups * group_dim
    tm, sub = TOKEN_TILE, SUB_TILE
    assert seq % tm == 0 and tm % sub == 0 and sub >= POOL_HALO and batch <= MOD_ROWS
    assert batch * seq // tm > IO_SLOTS
    assert max(POOL_WINDOWS) <= POOL_HALO and CONV_K - 1 <= CONV_HALO
    assert all(win == 2 ** (g + 1) for g, win in enumerate(POOL_WINDOWS))
    assert w1.shape[2] % FF_CHUNK == 0 and STAGE_ROWS % group_dim == 0
    assert COL_PIECE % conv_ch == 0 and pool_width == conv_ch and w_in.shape[2] % COL_PIECE == 0
    in_hbm = (w_ada, w_in, w_out, w1, w2)
    assert all(w.shape[1] % STAGE_ROWS == 0 and w.shape[2] % STAGE_COLS == 0 for w in in_hbm)
    bf16 = jnp.bfloat16
    row = lambda a: a.reshape(1, -1)
    x_rows = x.reshape(batch * seq, d)
    operands = (x_rows, c, w_ada, row(b_ada), w_in, conv_w, w_pool, row(pool_scale), w_out,
                row(ln1_g), row(ln1_b), w1, w2, row(ln2_g), row(ln2_b))
    in_specs = [pl.BlockSpec(memory_space=pl.ANY) if a is x_rows or any(a is w for w in in_hbm)
                else pl.BlockSpec(memory_space=pltpu.VMEM) for a in operands]
    return pl.pallas_call(
        functools.partial(_layer_kernel, alpha=alpha, sub=sub, layer=layer,
                          n_tiles=batch * seq // tm, tiles_per_seq=seq // tm),
        in_specs=in_specs,
        out_specs=pl.BlockSpec(memory_space=pl.ANY),
        out_shape=jax.ShapeDtypeStruct((batch * seq, d), x.dtype),
        scratch_shapes=[
            pltpu.VMEM((CONV_HALO, conv_ch), jnp.float32),
            pltpu.VMEM((POOL_HALO, pool_width), jnp.float32),
            pltpu.VMEM((pool_width, d), bf16),
            pltpu.VMEM((MOD_ROWS, w_ada.shape[2]), jnp.float32),
            pltpu.VMEM(w_in.shape[1:], bf16),
            pltpu.VMEM((conv_ch, d), bf16),
            pltpu.VMEM(w1.shape[1:], bf16),
            pltpu.VMEM(w2.shape[1:], bf16),
            pltpu.VMEM((STAGE_SLOTS, STAGE_ROWS, STAGE_COLS), jnp.float32),
            pltpu.SemaphoreType.DMA((STAGE_SLOTS,)),
            pltpu.VMEM((IO_SLOTS, tm, d), jnp.float32),
            pltpu.SemaphoreType.DMA((IO_SLOTS,)),
            pltpu.VMEM((IO_SLOTS, tm, d), jnp.float32),
            pltpu.SemaphoreType.DMA((IO_SLOTS,)),
            pltpu.VMEM((sub // LAST_OUT_PARTS, d), jnp.float32),
            pltpu.VMEM((sub // LAST_OUT_PARTS, d), jnp.float32),
        ],
        compiler_params=pltpu.CompilerParams(vmem_limit_bytes=VMEM_LIMIT_BYTES),
        name="hybrid_layer",
    )(*operands).reshape(x.shape)


def kernel(x, c, w_ada, b_ada, w_in, conv_w, w_pool, pool_scale, w_out, ln1_g, ln1_b, w_mlp_in,
           w_mlp_out, ln2_g, ln2_b):
    depth = w_in.shape[0]
    alpha = (2.0 * depth) ** 0.25
    for l in range(depth):
        x = _layer(x, c, w_ada, b_ada[l], w_in, conv_w[l], w_pool[l], pool_scale[l], w_out,
                   ln1_g[l], ln1_b[l], w_mlp_in, w_mlp_out, ln2_g[l], ln2_b[l], alpha=alpha,
                   layer=l)
    return x
```

```python
import functools

import jax
import jax.numpy as jnp
from jax import lax
from jax.experimental import pallas as pl
from jax.experimental.pallas import tpu as pltpu

CONV_K = 3
POOL_WINDOWS = (2, 4, 8, 16)
N_MOD = 6
LN_EPS = 1e-5

CONV_HALO = 8
POOL_HALO = 16
MOD_ROWS = 8

TOKEN_TILE = 1024
SUB_TILE = 512
COL_PIECE = 1024
FF_CHUNK = 1024
LAST_OUT_PARTS = 1
IO_SLOTS = 2
STAGE_SLOTS = 3
STAGE_ROWS = 256
STAGE_COLS = 1024
VMEM_LIMIT_BYTES = 60 * 1024 * 1024


def _layer_norm(r, g, b):
    mu = jnp.mean(r, axis=-1, keepdims=True)
    d = r - mu
    var = jnp.mean(d * d, axis=-1, keepdims=True)
    return d * lax.rsqrt(var + LN_EPS) * g + b


def _dot(a, b):
    return jnp.dot(a, b, preferred_element_type=jnp.float32)


def _layer_kernel(x_hbm, c_ref, w_ada_hbm, b_ada_ref, w_in_hbm, conv_w_ref, w_pool_ref,
                  pool_scale_ref, w_out_hbm, ln1_g_ref, ln1_b_ref, w1_hbm, w2_hbm, ln2_g_ref,
                  ln2_b_ref, o_hbm,
                  carry_u, carry_v, w_comb, mod_s, w_in_ref, w_out_ref, w1_ref, w2_ref, stage,
                  stage_sem, x_buf, x_sem, out_buf, out_sem, late_f, late_x1,
                  *, alpha, sub, layer, n_tiles, tiles_per_seq):
    tm, d = x_buf.shape[1], x_buf.shape[2]
    n_sub = tm // sub
    conv_ch = conv_w_ref.shape[1]
    n_groups, group_dim, _ = w_pool_ref.shape
    d_ff = w1_ref.shape[1]
    late = late_f.shape[0]
    n_in = w_in_ref.shape[1] // COL_PIECE
    n_ff = d_ff // FF_CHUNK

    def x_fetch(t):
        return pltpu.make_async_copy(x_hbm.at[pl.ds(t * tm, tm), :], x_buf.at[t % IO_SLOTS],
                                     x_sem.at[t % IO_SLOTS])

    def write_back(slot, buf_row0, rows, hbm_row0):
        return pltpu.make_async_copy(out_buf.at[slot, pl.ds(buf_row0, rows), :],
                                     o_hbm.at[pl.ds(hbm_row0, rows), :], out_sem.at[slot])

    x_fetch(0).start()
    late_f[...] = jnp.zeros(late_f.shape, late_f.dtype)
    late_x1[...] = jnp.zeros(late_x1.shape, late_x1.dtype)

    c = c_ref[...]
    cond = c * jax.nn.sigmoid(c)
    cond = jnp.concatenate(
        [cond, jnp.zeros((MOD_ROWS - c.shape[0], d), cond.dtype)], axis=0).astype(jnp.bfloat16)

    def into_mod(r0, c0, block):
        part = _dot(cond[:, r0:r0 + STAGE_ROWS], block.astype(jnp.bfloat16))
        base = b_ada_ref[:, c0:c0 + STAGE_COLS] if r0 == 0 else mod_s[:, c0:c0 + STAGE_COLS]
        mod_s[:, c0:c0 + STAGE_COLS] = base + part

    def into(dst):
        def keep(r0, c0, block):
            dst[r0:r0 + STAGE_ROWS, c0:c0 + STAGE_COLS] = block.astype(dst.dtype)
        return keep

    def into_out_proj(r0, c0, block):
        if r0 < conv_ch:
            into(w_out_ref)(r0, c0, block)
        for g in range(n_groups):
            g0 = conv_ch + g * group_dim
            if r0 <= g0 and g0 + group_dim <= r0 + STAGE_ROWS:
                scaled = w_pool_ref[g] * pool_scale_ref[:, g * group_dim:(g + 1) * group_dim]
                w_comb[g * group_dim:(g + 1) * group_dim, c0:c0 + STAGE_COLS] = jnp.dot(
                    scaled, block[g0 - r0:g0 - r0 + group_dim, :],
                    precision=lax.Precision.HIGHEST,
                    preferred_element_type=jnp.float32).astype(w_comb.dtype)

    blocks = [(hbm, r0, c0, consume)
              for hbm, shape, consume in (
                  (w_ada_hbm, (d, mod_s.shape[1]), into_mod),
                  (w_in_hbm, w_in_ref.shape, into(w_in_ref)),
                  (w_out_hbm, (d, d), into_out_proj),
                  (w1_hbm, w1_ref.shape, into(w1_ref)),
                  (w2_hbm, w2_ref.shape, into(w2_ref)))
              for r0 in range(0, shape[0], STAGE_ROWS)
              for c0 in range(0, shape[1], STAGE_COLS)]

    def fetch(i):
        hbm, r0, c0, _ = blocks[i]
        return pltpu.make_async_copy(
            hbm.at[layer, pl.ds(r0, STAGE_ROWS), pl.ds(c0, STAGE_COLS)],
            stage.at[i % STAGE_SLOTS], stage_sem.at[i % STAGE_SLOTS])

    ahead = STAGE_SLOTS - 1
    for i in range(min(ahead, len(blocks))):
        fetch(i).start()
    for i, (_, r0, c0, consume) in enumerate(blocks):
        if i + ahead < len(blocks):
            fetch(i + ahead).start()
        fetch(i).wait()
        consume(r0, c0, stage[i % STAGE_SLOTS])

    cw = conv_w_ref[...]

    def finish(x1, f, gate, slot, buf_row0):
        out_buf[slot, pl.ds(buf_row0, f.shape[0]), :] = _layer_norm(
            alpha * x1 + (1.0 + gate) * f, ln2_g_ref[...], ln2_b_ref[...])

    def late_norm2(slot, owner_tile):
        gate = mod_s[pl.ds(owner_tile // tiles_per_seq, 1), (N_MOD - 1) * d:]
        finish(late_x1[...], late_f[...], gate, slot, 0)

    def tile_body(t, _):
        slot = t % IO_SLOTS
        j = t % tiles_per_seq

        x_fetch(t).wait()

        @pl.when(t + 1 < n_tiles)
        def _():
            x_fetch(t + 1).start()

        @pl.when(t == IO_SLOTS)
        def _():
            write_back(slot, late, tm - late, 0).wait()

        @pl.when(t > IO_SLOTS)
        def _():
            write_back(slot, 0, tm, 0).wait()

        @pl.when(j == 0)
        def _():
            carry_u[...] = jnp.zeros(carry_u.shape, carry_u.dtype)
            carry_v[...] = jnp.zeros(carry_v.shape, carry_v.dtype)

        mod = mod_s[pl.ds(t // tiles_per_seq, 1), :]
        sh1, sc1, g1, sh2, sc2, g2 = (mod[:, i * d:(i + 1) * d] for i in range(N_MOD))

        def load(s, st):
            x = x_buf[slot, pl.ds(s * sub, sub), :]
            st["h"] = (x * (1.0 + sc1) + sh1).astype(jnp.bfloat16)
            st["z"] = {}

        def in_proj(s, st, n):
            piece = _dot(st["h"], w_in_ref[:, n * COL_PIECE:(n + 1) * COL_PIECE])
            for k in range(COL_PIECE // conv_ch):
                st["z"][n * (COL_PIECE // conv_ch) + k] = piece[:, k * conv_ch:(k + 1) * conv_ch]

        def with_history(s, st, key, cur, carry, halo):
            st[key] = cur[sub - halo:, :]
            if s == n_sub - 1:
                carry[...] = st[key]
            return jnp.concatenate([carry[...] if s == 0 else subs[s - 1][key], cur], axis=0)

        def conv(s, st):
            u = st["z"][1] * st["z"][2]
            ext = with_history(s, st, "u_tail", u, carry_u, CONV_HALO)
            acc = u * cw[CONV_K - 1:CONV_K]
            for k in range(CONV_K - 1):
                shifted = pltpu.roll(ext, CONV_K - 1 - k, axis=0)[CONV_HALO:, :]
                acc = acc + shifted * cw[k:k + 1]
            st["y_conv"] = (st["z"][0] * acc).astype(jnp.bfloat16)

        def pool(s, st):
            v_pool = st["z"][3]
            ext = with_history(s, st, "v_tail", v_pool, carry_v, POOL_HALO)
            sums, part = [], ext
            for g, win in enumerate(POOL_WINDOWS):
                part = part + pltpu.roll(part, win // 2, axis=0)
                sums.append(part[POOL_HALO:, 0:group_dim])
                part = part[:, group_dim:]
            pos = (j * tm + s * sub + 1
                   + lax.broadcasted_iota(jnp.int32, (sub, 1), 0)).astype(jnp.float32)
            pooled = [
                sums[g] / jnp.minimum(pos, float(win))
                - v_pool[:, g * group_dim:(g + 1) * group_dim]
                for g, win in enumerate(POOL_WINDOWS)]
            st["p_in"] = jnp.concatenate(pooled, axis=1).astype(jnp.bfloat16)

        def out_proj(s, st, k):
            if k == 0:
                st["mix"] = _dot(st["y_conv"], w_out_ref[0:conv_ch, :])
            else:
                st["mix"] = st["mix"] + _dot(st["p_in"], w_comb[...])

        def norm1(s, st):
            x = x_buf[slot, pl.ds(s * sub, sub), :]
            x1 = _layer_norm(alpha * x + (1.0 + g1) * st["mix"], ln1_g_ref[...], ln1_b_ref[...])
            st["x1"] = x1
            st["h2"] = (x1 * (1.0 + sc2) + sh2).astype(jnp.bfloat16)

        def mlp_in(s, st, c):
            c0 = c * FF_CHUNK
            hid = _dot(st["h2"], w1_ref[:, c0:c0 + FF_CHUNK])
            st.setdefault("hid", []).append(
                jnp.square(jnp.maximum(hid, 0.0)).astype(jnp.bfloat16))

        def mlp_out(s, st, parts=1):
            hid = jnp.concatenate(st["hid"], axis=1)
            rows = sub // parts
            st["f"] = [_dot(hid[r:r + rows, :], w2_ref[...]) for r in range(0, sub, rows)]

        def norm2(s, st, defer_last=False):
            rows = sub // len(st["f"])
            for i, f in enumerate(st["f"]):
                x1 = st["x1"][i * rows:(i + 1) * rows]
                if defer_last and i == len(st["f"]) - 1:
                    late_f[...] = f
                    late_x1[...] = x1
                else:
                    finish(x1, f, g2, slot, late + s * sub + i * rows)

        assert n_sub == 2 and n_in in (2, 4)
        subs = [{} for _ in range(n_sub)]
        a, b = subs
        load(0, a)
        late_norm2(slot, jnp.maximum(t - 1, 0))
        for n in range(n_in):
            in_proj(0, a, n)
        load(1, b)
        in_proj(1, b, 0)
        conv(0, a)
        if n_in == 4:
            in_proj(1, b, 1)
        pool(0, a)
        for n in range(n_in // 2, n_in):
            in_proj(1, b, n)
        out_proj(0, a, 0)
        conv(1, b)
        out_proj(0, a, 1)
        pool(1, b)
        out_proj(1, b, 0)
        norm1(0, a)
        out_proj(1, b, 1)
        mlp_in(0, a, 0)
        norm1(1, b)
        for c in range(1, n_ff):
            mlp_in(0, a, c)
        mlp_out(0, a)
        mlp_in(1, b, 0)
        norm2(0, a)
        for c in range(1, n_ff):
            mlp_in(1, b, c)
        mlp_out(1, b, parts=LAST_OUT_PARTS)
        norm2(1, b, defer_last=True)

        @pl.when(t == 0)
        def _():
            write_back(slot, late, tm - late, 0).start()

        @pl.when(t > 0)
        def _():
            write_back(slot, 0, tm, t * tm - late).start()

        return 0

    lax.fori_loop(0, n_tiles, tile_body, 0)

    last = n_tiles - 1
    other = (last + 1) % IO_SLOTS
    write_back(other, 0, tm, 0).wait()
    late_norm2(other, last)
    tail = write_back(other, 0, late, n_tiles * tm - late)
    tail.start()
    tail.wait()
    write_back(last % IO_SLOTS, 0, tm, 0).wait()


def _layer(x, c, w_ada, b_ada, w_in, conv_w, w_pool, pool_scale, w_out, ln1_g, ln1_b, w1, w2,
           ln2_g, ln2_b, *, alpha, layer):
    batch, seq, d = x.shape
    conv_ch = conv_w.shape[1]
    n_groups, group_dim, _ = w_pool.shape
    pool_width = n_groups * group_dim
    tm, sub = TOKEN_TILE, SUB_TILE
    assert seq % tm == 0 and tm % sub == 0 and sub >= POOL_HALO and batch <= MOD_ROWS
    assert batch * seq // tm > IO_SLOTS
    assert max(POOL_WINDOWS) <= POOL_HALO and CONV_K - 1 <= CONV_HALO
    assert all(win == 2 ** (g + 1) for g, win in enumerate(POOL_WINDOWS))
    assert w1.shape[2] % FF_CHUNK == 0 and STAGE_ROWS % group_dim == 0
    assert COL_PIECE % conv_ch == 0 and pool_width == conv_ch and w_in.shape[2] % COL_PIECE == 0
    in_hbm = (w_ada, w_in, w_out, w1, w2)
    assert all(w.shape[1] % STAGE_ROWS == 0 and w.shape[2] % STAGE_COLS == 0 for w in in_hbm)
    bf16 = jnp.bfloat16
    row = lambda a: a.reshape(1, -1)
    x_rows = x.reshape(batch * seq, d)
    operands = (x_rows, c, w_ada, row(b_ada), w_in, conv_w, w_pool, row(pool_scale), w_out,
                row(ln1_g), row(ln1_b), w1, w2, row(ln2_g), row(ln2_b))
    in_specs = [pl.BlockSpec(memory_space=pl.ANY) if a is x_rows or any(a is w for w in in_hbm)
                else pl.BlockSpec(memory_space=pltpu.VMEM) for a in operands]
    return pl.pallas_call(
        functools.partial(_layer_kernel, alpha=alpha, sub=sub, layer=layer,
                          n_tiles=batch * seq // tm, tiles_per_seq=seq // tm),
        in_specs=in_specs,
        out_specs=pl.BlockSpec(memory_space=pl.ANY),
        out_shape=jax.ShapeDtypeStruct((batch * seq, d), x.dtype),
        scratch_shapes=[
            pltpu.VMEM((CONV_HALO, conv_ch), jnp.float32),
            pltpu.VMEM((POOL_HALO, pool_width), jnp.float32),
            pltpu.VMEM((pool_width, d), bf16),
            pltpu.VMEM((MOD_ROWS, w_ada.shape[2]), jnp.float32),
            pltpu.VMEM(w_in.shape[1:], bf16),
            pltpu.VMEM((conv_ch, d), bf16),
            pltpu.VMEM(w1.shape[1:], bf16),
            pltpu.VMEM(w2.shape[1:], bf16),
            pltpu.VMEM((STAGE_SLOTS, STAGE_ROWS, STAGE_COLS), jnp.float32),
            pltpu.SemaphoreType.DMA((STAGE_SLOTS,)),
            pltpu.VMEM((IO_SLOTS, tm, d), jnp.float32),
            pltpu.SemaphoreType.DMA((IO_SLOTS,)),
            pltpu.VMEM((IO_SLOTS, tm, d), jnp.float32),
            pltpu.SemaphoreType.DMA((IO_SLOTS,)),
            pltpu.VMEM((sub // LAST_OUT_PARTS, d), jnp.float32),
            pltpu.VMEM((sub // LAST_OUT_PARTS, d), jnp.float32),
        ],
        compiler_params=pltpu.CompilerParams(vmem_limit_bytes=VMEM_LIMIT_BYTES),
        name="hybrid_layer",
    )(*operands).reshape(x.shape)


def kernel(x, c, w_ada, b_ada, w_in, conv_w, w_pool, pool_scale, w_out, ln1_g, ln1_b, w_mlp_in,
           w_mlp_out, ln2_g, ln2_b):
    depth = w_in.shape[0]
    alpha = (2.0 * depth) ** 0.25
    for l in range(depth):
        x = _layer(x, c, w_ada, b_ada[l], w_in, conv_w[l], w_pool[l], pool_scale[l], w_out,
                   ln1_g[l], ln1_b[l], w_mlp_in, w_mlp_out, ln2_g[l], ln2_b[l], alpha=alpha,
                   layer=l)
    return x
```

```python
import functools

import jax
import jax.numpy as jnp
from jax import lax
from jax.experimental import pallas as pl
from jax.experimental.pallas import tpu as pltpu

CONV_K = 3
POOL_WINDOWS = (2, 4, 8, 16)
N_MOD = 6
LN_EPS = 1e-5

CONV_HALO = 8
POOL_HALO = 16
MOD_ROWS = 8

TOKEN_TILE = 1024
SUB_TILE = 512
COL_PIECE = 1024
FF_CHUNK = 1024
LAST_OUT_PARTS = 2
IO_SLOTS = 2
STAGE_SLOTS = 3
STAGE_ROWS = 512
STAGE_COLS = 1024
VMEM_LIMIT_BYTES = 60 * 1024 * 1024


def _layer_norm(r, g, b):
    mu = jnp.mean(r, axis=-1, keepdims=True)
    d = r - mu
    var = jnp.mean(d * d, axis=-1, keepdims=True)
    return d * lax.rsqrt(var + LN_EPS) * g + b


def _dot(a, b):
    return jnp.dot(a, b, preferred_element_type=jnp.float32)


def _layer_kernel(x_hbm, c_ref, w_ada_hbm, b_ada_ref, w_in_hbm, conv_w_ref, w_pool_ref,
                  pool_scale_ref, w_out_hbm, ln1_g_ref, ln1_b_ref, w1_hbm, w2_hbm, ln2_g_ref,
                  ln2_b_ref, o_hbm,
                  carry_u, carry_v, w_comb, mod_s, w_in_ref, w_out_ref, w1_ref, w2_ref, stage,
                  stage_sem, x_buf, x_sem, out_buf, out_sem, late_f, late_x1,
                  *, alpha, sub, layer, n_tiles, tiles_per_seq):
    tm, d = x_buf.shape[1], x_buf.shape[2]
    n_sub = tm // sub
    conv_ch = conv_w_ref.shape[1]
    n_groups, group_dim, _ = w_pool_ref.shape
    d_ff = w1_ref.shape[1]
    late = late_f.shape[0]
    n_in = w_in_ref.shape[1] // COL_PIECE
    n_ff = d_ff // FF_CHUNK

    def x_fetch(t):
        return pltpu.make_async_copy(x_hbm.at[pl.ds(t * tm, tm), :], x_buf.at[t % IO_SLOTS],
                                     x_sem.at[t % IO_SLOTS])

    def write_back(slot, buf_row0, rows, hbm_row0):
        return pltpu.make_async_copy(out_buf.at[slot, pl.ds(buf_row0, rows), :],
                                     o_hbm.at[pl.ds(hbm_row0, rows), :], out_sem.at[slot])

    x_fetch(0).start()
    late_f[...] = jnp.zeros(late_f.shape, late_f.dtype)
    late_x1[...] = jnp.zeros(late_x1.shape, late_x1.dtype)

    c = c_ref[...]
    cond = c * jax.nn.sigmoid(c)
    cond = jnp.concatenate(
        [cond, jnp.zeros((MOD_ROWS - c.shape[0], d), cond.dtype)], axis=0).astype(jnp.bfloat16)

    def into_mod(r0, c0, block):
        part = _dot(cond[:, r0:r0 + STAGE_ROWS], block.astype(jnp.bfloat16))
        base = b_ada_ref[:, c0:c0 + STAGE_COLS] if r0 == 0 else mod_s[:, c0:c0 + STAGE_COLS]
        mod_s[:, c0:c0 + STAGE_COLS] = base + part

    def into(dst):
        def keep(r0, c0, block):
            dst[r0:r0 + STAGE_ROWS, c0:c0 + STAGE_COLS] = block.astype(dst.dtype)
        return keep

    def into_out_proj(r0, c0, block):
        if r0 < conv_ch:
            into(w_out_ref)(r0, c0, block)
        for g in range(n_groups):
            g0 = conv_ch + g * group_dim
            if r0 <= g0 and g0 + group_dim <= r0 + STAGE_ROWS:
                scaled = w_pool_ref[g] * pool_scale_ref[:, g * group_dim:(g + 1) * group_dim]
                w_comb[g * group_dim:(g + 1) * group_dim, c0:c0 + STAGE_COLS] = jnp.dot(
                    scaled, block[g0 - r0:g0 - r0 + group_dim, :],
                    precision=lax.Precision.HIGHEST,
                    preferred_element_type=jnp.float32).astype(w_comb.dtype)

    blocks = [(hbm, r0, c0, consume)
              for hbm, shape, consume in (
                  (w_ada_hbm, (d, mod_s.shape[1]), into_mod),
                  (w_in_hbm, w_in_ref.shape, into(w_in_ref)),
                  (w_out_hbm, (d, d), into_out_proj),
                  (w1_hbm, w1_ref.shape, into(w1_ref)),
                  (w2_hbm, w2_ref.shape, into(w2_ref)))
              for r0 in range(0, shape[0], STAGE_ROWS)
              for c0 in range(0, shape[1], STAGE_COLS)]

    assert tm % STAGE_ROWS == 0 and d == STAGE_COLS
    slots = [stage.at[k] for k in range(STAGE_SLOTS)]
    slots += [out_buf.at[k, pl.ds(r, STAGE_ROWS), :]
              for k in range(IO_SLOTS) for r in range(0, tm, STAGE_ROWS)]

    def fetch(i):
        hbm, r0, c0, _ = blocks[i]
        return pltpu.make_async_copy(
            hbm.at[layer, pl.ds(r0, STAGE_ROWS), pl.ds(c0, STAGE_COLS)],
            slots[i % len(slots)], stage_sem.at[i % len(slots)])

    ahead = len(slots) - 1
    for i in range(min(ahead, len(blocks))):
        fetch(i).start()
    for i, (_, r0, c0, consume) in enumerate(blocks):
        if i + ahead < len(blocks):
            fetch(i + ahead).start()
        fetch(i).wait()
        consume(r0, c0, slots[i % len(slots)][...])

    cw = conv_w_ref[...]

    def finish(x1, f, gate, slot, buf_row0):
        out_buf[slot, pl.ds(buf_row0, f.shape[0]), :] = _layer_norm(
            alpha * x1 + (1.0 + gate) * f, ln2_g_ref[...], ln2_b_ref[...])

    def late_norm2(slot, owner_tile):
        gate = mod_s[pl.ds(owner_tile // tiles_per_seq, 1), (N_MOD - 1) * d:]
        finish(late_x1[...], late_f[...], gate, slot, 0)

    def tile_body(t, _):
        slot = t % IO_SLOTS
        j = t % tiles_per_seq

        x_fetch(t).wait()

        @pl.when(t + 1 < n_tiles)
        def _():
            x_fetch(t + 1).start()

        @pl.when(t == IO_SLOTS)
        def _():
            write_back(slot, late, tm - late, 0).wait()

        @pl.when(t > IO_SLOTS)
        def _():
            write_back(slot, 0, tm, 0).wait()

        @pl.when(j == 0)
        def _():
            carry_u[...] = jnp.zeros(carry_u.shape, carry_u.dtype)
            carry_v[...] = jnp.zeros(carry_v.shape, carry_v.dtype)

        mod = mod_s[pl.ds(t // tiles_per_seq, 1), :]
        sh1, sc1, g1, sh2, sc2, g2 = (mod[:, i * d:(i + 1) * d] for i in range(N_MOD))

        def load(s, st):
            x = x_buf[slot, pl.ds(s * sub, sub), :]
            st["h"] = (x * (1.0 + sc1) + sh1).astype(jnp.bfloat16)
            st["z"] = {}

        def in_proj(s, st, n):
            piece = _dot(st["h"], w_in_ref[:, n * COL_PIECE:(n + 1) * COL_PIECE])
            for k in range(COL_PIECE // conv_ch):
                st["z"][n * (COL_PIECE // conv_ch) + k] = piece[:, k * conv_ch:(k + 1) * conv_ch]

        def with_history(s, st, key, cur, carry, halo):
            st[key] = cur[sub - halo:, :]
            if s == n_sub - 1:
                carry[...] = st[key]
            return jnp.concatenate([carry[...] if s == 0 else subs[s - 1][key], cur], axis=0)

        def conv(s, st):
            u = st["z"][1] * st["z"][2]
            ext = with_history(s, st, "u_tail", u, carry_u, CONV_HALO)
            acc = u * cw[CONV_K - 1:CONV_K]
            for k in range(CONV_K - 1):
                shifted = pltpu.roll(ext, CONV_K - 1 - k, axis=0)[CONV_HALO:, :]
                acc = acc + shifted * cw[k:k + 1]
            st["y_conv"] = (st["z"][0] * acc).astype(jnp.bfloat16)

        def pool(s, st):
            v_pool = st["z"][3]
            ext = with_history(s, st, "v_tail", v_pool, carry_v, POOL_HALO)
            sums, part = [], ext
            for g, win in enumerate(POOL_WINDOWS):
                part = part + pltpu.roll(part, win // 2, axis=0)
                sums.append(part[POOL_HALO:, 0:group_dim])
                part = part[:, group_dim:]
            pos = (j * tm + s * sub + 1
                   + lax.broadcasted_iota(jnp.int32, (sub, 1), 0)).astype(jnp.float32)
            pooled = [
                sums[g] / jnp.minimum(pos, float(win))
                - v_pool[:, g * group_dim:(g + 1) * group_dim]
                for g, win in enumerate(POOL_WINDOWS)]
            st["p_in"] = jnp.concatenate(pooled, axis=1).astype(jnp.bfloat16)

        def out_proj(s, st, k):
            if k == 0:
                st["mix"] = _dot(st["y_conv"], w_out_ref[0:conv_ch, :])
            else:
                st["mix"] = st["mix"] + _dot(st["p_in"], w_comb[...])

        def norm1(s, st):
            x = x_buf[slot, pl.ds(s * sub, sub), :]
            x1 = _layer_norm(alpha * x + (1.0 + g1) * st["mix"], ln1_g_ref[...], ln1_b_ref[...])
            st["x1"] = x1
            st["h2"] = (x1 * (1.0 + sc2) + sh2).astype(jnp.bfloat16)

        def mlp_in(s, st, c):
            c0 = c * FF_CHUNK
            hid = _dot(st["h2"], w1_ref[:, c0:c0 + FF_CHUNK])
            st.setdefault("hid", []).append(
                jnp.square(jnp.maximum(hid, 0.0)).astype(jnp.bfloat16))

        def mlp_out(s, st, parts=1):
            hid = jnp.concatenate(st["hid"], axis=1)
            rows = sub // parts
            st["f"] = [_dot(hid[r:r + rows, :], w2_ref[...]) for r in range(0, sub, rows)]

        def norm2(s, st, defer_last=False):
            rows = sub // len(st["f"])
            for i, f in enumerate(st["f"]):
                x1 = st["x1"][i * rows:(i + 1) * rows]
                if defer_last and i == len(st["f"]) - 1:
                    late_f[...] = f
                    late_x1[...] = x1
                else:
                    finish(x1, f, g2, slot, late + s * sub + i * rows)

        assert n_sub == 2 and n_in in (2, 4)
        subs = [{} for _ in range(n_sub)]
        a, b = subs
        load(0, a)
        late_norm2(slot, jnp.maximum(t - 1, 0))
        for n in range(n_in):
            in_proj(0, a, n)
        load(1, b)
        in_proj(1, b, 0)
        conv(0, a)
        if n_in == 4:
            in_proj(1, b, 1)
        pool(0, a)
        for n in range(n_in // 2, n_in):
            in_proj(1, b, n)
        out_proj(0, a, 0)
        conv(1, b)
        out_proj(0, a, 1)
        pool(1, b)
        out_proj(1, b, 0)
        norm1(0, a)
        out_proj(1, b, 1)
        mlp_in(0, a, 0)
        norm1(1, b)
        for c in range(1, n_ff):
            mlp_in(0, a, c)
        mlp_out(0, a)
        mlp_in(1, b, 0)
        norm2(0, a)
        for c in range(1, n_ff):
            mlp_in(1, b, c)
        mlp_out(1, b, parts=LAST_OUT_PARTS)
        norm2(1, b, defer_last=True)

        @pl.when(t == 0)
        def _():
            write_back(slot, late, tm - late, 0).start()

        @pl.when(t > 0)
        def _():
            write_back(slot, 0, tm, t * tm - late).start()

        return 0

    lax.fori_loop(0, n_tiles, tile_body, 0)

    last = n_tiles - 1
    other = (last + 1) % IO_SLOTS
    write_back(other, 0, tm, 0).wait()
    late_norm2(other, last)
    tail = write_back(other, 0, late, n_tiles * tm - late)
    tail.start()
    tail.wait()
    write_back(last % IO_SLOTS, 0, tm, 0).wait()


def _layer(x, c, w_ada, b_ada, w_in, conv_w, w_pool, pool_scale, w_out, ln1_g, ln1_b, w1, w2,
           ln2_g, ln2_b, *, alpha, layer):
    batch, seq, d = x.shape
    conv_ch = conv_w.shape[1]
    n_groups, group_dim, _ = w_pool.shape
    pool_width = n_groups * group_dim
    tm, sub = TOKEN_TILE, SUB_TILE
    assert seq % tm == 0 and tm % sub == 0 and sub >= POOL_HALO and batch <= MOD_ROWS
    assert batch * seq // tm > IO_SLOTS
    assert max(POOL_WINDOWS) <= POOL_HALO and CONV_K - 1 <= CONV_HALO
    assert all(win == 2 ** (g + 1) for g, win in enumerate(POOL_WINDOWS))
    assert w1.shape[2] % FF_CHUNK == 0 and STAGE_ROWS % group_dim == 0
    assert COL_PIECE % conv_ch == 0 and pool_width == conv_ch and w_in.shape[2] % COL_PIECE == 0
    in_hbm = (w_ada, w_in, w_out, w1, w2)
    assert all(w.shape[1] % STAGE_ROWS == 0 and w.shape[2] % STAGE_COLS == 0 for w in in_hbm)
    bf16 = jnp.bfloat16
    row = lambda a: a.reshape(1, -1)
    x_rows = x.reshape(batch * seq, d)
    operands = (x_rows, c, w_ada, row(b_ada), w_in, conv_w, w_pool, row(pool_scale), w_out,
                row(ln1_g), row(ln1_b), w1, w2, row(ln2_g), row(ln2_b))
    in_specs = [pl.BlockSpec(memory_space=pl.ANY) if a is x_rows or any(a is w for w in in_hbm)
                else pl.BlockSpec(memory_space=pltpu.VMEM) for a in operands]
    return pl.pallas_call(
        functools.partial(_layer_kernel, alpha=alpha, sub=sub, layer=layer,
                          n_tiles=batch * seq // tm, tiles_per_seq=seq // tm),
        in_specs=in_specs,
        out_specs=pl.BlockSpec(memory_space=pl.ANY),
        out_shape=jax.ShapeDtypeStruct((batch * seq, d), x.dtype),
        scratch_shapes=[
            pltpu.VMEM((CONV_HALO, conv_ch), jnp.float32),
            pltpu.VMEM((POOL_HALO, pool_width), jnp.float32),
            pltpu.VMEM((pool_width, d), bf16),
            pltpu.VMEM((MOD_ROWS, w_ada.shape[2]), jnp.float32),
            pltpu.VMEM(w_in.shape[1:], bf16),
            pltpu.VMEM((conv_ch, d), bf16),
            pltpu.VMEM(w1.shape[1:], bf16),
            pltpu.VMEM(w2.shape[1:], bf16),
            pltpu.VMEM((STAGE_SLOTS, STAGE_ROWS, STAGE_COLS), jnp.float32),
            pltpu.SemaphoreType.DMA((STAGE_SLOTS + IO_SLOTS * (tm // STAGE_ROWS),)),
            pltpu.VMEM((IO_SLOTS, tm, d), jnp.float32),
            pltpu.SemaphoreType.DMA((IO_SLOTS,)),
            pltpu.VMEM((IO_SLOTS, tm, d), jnp.float32),
            pltpu.SemaphoreType.DMA((IO_SLOTS,)),
            pltpu.VMEM((sub // LAST_OUT_PARTS, d), jnp.float32),
            pltpu.VMEM((sub // LAST_OUT_PARTS, d), jnp.float32),
        ],
        compiler_params=pltpu.CompilerParams(vmem_limit_bytes=VMEM_LIMIT_BYTES),
        name="hybrid_layer",
    )(*operands).reshape(x.shape)


def kernel(x, c, w_ada, b_ada, w_in, conv_w, w_pool, pool_scale, w_out, ln1_g, ln1_b, w_mlp_in,
           w_mlp_out, ln2_g, ln2_b):
    depth = w_in.shape[0]
    alpha = (2.0 * depth) ** 0.25
    for l in range(depth):
        x = _layer(x, c, w_ada, b_ada[l], w_in, conv_w[l], w_pool[l], pool_scale[l], w_out,
                   ln1_g[l], ln1_b[l], w_mlp_in, w_mlp_out, ln2_g[l], ln2_b[l], alpha=alpha,
                   layer=l)
    return x
```

```python
import functools

import jax
import jax.numpy as jnp
from jax import lax
from jax.experimental import pallas as pl
from jax.experimental.pallas import tpu as pltpu

CONV_K = 3
POOL_WINDOWS = (2, 4, 8, 16)
N_MOD = 6
LN_EPS = 1e-5

CONV_HALO = 8
POOL_HALO = 16
MOD_ROWS = 8

TOKEN_TILE = 1024
SUB_TILE = 512
COL_PIECE = 1024
FF_CHUNK = 1024
LAST_OUT_PARTS = 1
IO_SLOTS = 2
STAGE_ROWS = 512
STAGE_COLS = 1024
VMEM_LIMIT_BYTES = 60 * 1024 * 1024


def _layer_norm(r, g, b):
    mu = jnp.mean(r, axis=-1, keepdims=True)
    d = r - mu
    var = jnp.mean(d * d, axis=-1, keepdims=True)
    return d * lax.rsqrt(var + LN_EPS) * g + b


def _dot(a, b):
    return jnp.dot(a, b, preferred_element_type=jnp.float32)


def _layer_kernel(x_hbm, c_ref, w_ada_hbm, b_ada_ref, w_in_hbm, conv_w_ref, w_pool_ref,
                  pool_scale_ref, w_out_hbm, ln1_g_ref, ln1_b_ref, w1_hbm, w2_hbm, ln2_g_ref,
                  ln2_b_ref, o_hbm,
                  carry_u, carry_v, w_comb, mod_s, w_in_ref, w_out_ref, w1_ref, w2_ref,
                  stage_sem, x_buf, x_sem, out_buf, out_sem, late_f, late_x1,
                  *, alpha, sub, layer, n_tiles, tiles_per_seq):
    tm, d = x_buf.shape[1], x_buf.shape[2]
    n_sub = tm // sub
    conv_ch = conv_w_ref.shape[1]
    n_groups, group_dim, _ = w_pool_ref.shape
    d_ff = w1_ref.shape[1]
    late = late_f.shape[0]
    n_in = w_in_ref.shape[1] // COL_PIECE
    n_ff = d_ff // FF_CHUNK

    def x_fetch(t):
        return pltpu.make_async_copy(x_hbm.at[pl.ds(t * tm, tm), :], x_buf.at[t % IO_SLOTS],
                                     x_sem.at[t % IO_SLOTS])

    def write_back(slot, buf_row0, rows, hbm_row0):
        return pltpu.make_async_copy(out_buf.at[slot, pl.ds(buf_row0, rows), :],
                                     o_hbm.at[pl.ds(hbm_row0, rows), :], out_sem.at[slot])

    x_fetch(0).start()
    late_f[...] = jnp.zeros(late_f.shape, late_f.dtype)
    late_x1[...] = jnp.zeros(late_x1.shape, late_x1.dtype)

    c = c_ref[...]
    cond = c * jax.nn.sigmoid(c)
    cond = jnp.concatenate(
        [cond, jnp.zeros((MOD_ROWS - c.shape[0], d), cond.dtype)], axis=0).astype(jnp.bfloat16)

    def into_mod(r0, c0, block):
        part = _dot(cond[:, r0:r0 + STAGE_ROWS], block.astype(jnp.bfloat16))
        base = b_ada_ref[:, c0:c0 + STAGE_COLS] if r0 == 0 else mod_s[:, c0:c0 + STAGE_COLS]
        mod_s[:, c0:c0 + STAGE_COLS] = base + part

    def into(dst):
        def keep(r0, c0, block):
            dst[r0:r0 + STAGE_ROWS, c0:c0 + STAGE_COLS] = block.astype(dst.dtype)
        return keep

    def into_out_proj(r0, c0, block):
        if r0 < conv_ch:
            into(w_out_ref)(r0, c0, block)
        for g in range(n_groups):
            g0 = conv_ch + g * group_dim
            if r0 <= g0 and g0 + group_dim <= r0 + STAGE_ROWS:
                scaled = w_pool_ref[g] * pool_scale_ref[:, g * group_dim:(g + 1) * group_dim]
                w_comb[g * group_dim:(g + 1) * group_dim, c0:c0 + STAGE_COLS] = jnp.dot(
                    scaled, block[g0 - r0:g0 - r0 + group_dim, :],
                    precision=lax.Precision.HIGHEST,
                    preferred_element_type=jnp.float32).astype(w_comb.dtype)

    blocks = [(hbm, r0, c0, consume)
              for hbm, shape, consume in (
                  (w_ada_hbm, (d, mod_s.shape[1]), into_mod),
                  (w_in_hbm, w_in_ref.shape, into(w_in_ref)),
                  (w_out_hbm, (d, d), into_out_proj),
                  (w1_hbm, w1_ref.shape, into(w1_ref)),
                  (w2_hbm, w2_ref.shape, into(w2_ref)))
              for r0 in range(0, shape[0], STAGE_ROWS)
              for c0 in range(0, shape[1], STAGE_COLS)]

    assert tm % STAGE_ROWS == 0 and d == STAGE_COLS
    slots = [buf.at[k, pl.ds(r, STAGE_ROWS), :]
             for buf, first in ((out_buf, 0), (x_buf, 1))
             for k in range(first, IO_SLOTS) for r in range(0, tm, STAGE_ROWS)]

    def fetch(i):
        hbm, r0, c0, _ = blocks[i]
        return pltpu.make_async_copy(
            hbm.at[layer, pl.ds(r0, STAGE_ROWS), pl.ds(c0, STAGE_COLS)],
            slots[i % len(slots)], stage_sem.at[i % len(slots)])

    ahead = len(slots) - 1
    for i in range(min(ahead, len(blocks))):
        fetch(i).start()
    for i, (_, r0, c0, consume) in enumerate(blocks):
        if i + ahead < len(blocks):
            fetch(i + ahead).start()
        fetch(i).wait()
        consume(r0, c0, slots[i % len(slots)][...])

    cw = conv_w_ref[...]

    def finish(x1, f, gate, slot, buf_row0):
        out_buf[slot, pl.ds(buf_row0, f.shape[0]), :] = _layer_norm(
            alpha * x1 + (1.0 + gate) * f, ln2_g_ref[...], ln2_b_ref[...])

    def late_norm2(slot, owner_tile):
        gate = mod_s[pl.ds(owner_tile // tiles_per_seq, 1), (N_MOD - 1) * d:]
        finish(late_x1[...], late_f[...], gate, slot, 0)

    def tile_body(t, _):
        slot = t % IO_SLOTS
        j = t % tiles_per_seq

        x_fetch(t).wait()

        @pl.when(t + 1 < n_tiles)
        def _():
            x_fetch(t + 1).start()

        @pl.when(t == IO_SLOTS)
        def _():
            write_back(slot, late, tm - late, 0).wait()

        @pl.when(t > IO_SLOTS)
        def _():
            write_back(slot, 0, tm, 0).wait()

        @pl.when(j == 0)
        def _():
            carry_u[...] = jnp.zeros(carry_u.shape, carry_u.dtype)
            carry_v[...] = jnp.zeros(carry_v.shape, carry_v.dtype)

        mod = mod_s[pl.ds(t // tiles_per_seq, 1), :]
        sh1, sc1, g1, sh2, sc2, g2 = (mod[:, i * d:(i + 1) * d] for i in range(N_MOD))

        def load(s, st):
            x = x_buf[slot, pl.ds(s * sub, sub), :]
            st["h"] = (x * (1.0 + sc1) + sh1).astype(jnp.bfloat16)
            st["z"] = {}

        def in_proj(s, st, n):
            piece = _dot(st["h"], w_in_ref[:, n * COL_PIECE:(n + 1) * COL_PIECE])
            for k in range(COL_PIECE // conv_ch):
                st["z"][n * (COL_PIECE // conv_ch) + k] = piece[:, k * conv_ch:(k + 1) * conv_ch]

        def with_history(s, st, key, cur, carry, halo):
            st[key] = cur[sub - halo:, :]
            if s == n_sub - 1:
                carry[...] = st[key]
            return jnp.concatenate([carry[...] if s == 0 else subs[s - 1][key], cur], axis=0)

        def conv(s, st):
            u = st["z"][1] * st["z"][2]
            ext = with_history(s, st, "u_tail", u, carry_u, CONV_HALO)
            acc = u * cw[CONV_K - 1:CONV_K]
            for k in range(CONV_K - 1):
                shifted = pltpu.roll(ext, CONV_K - 1 - k, axis=0)[CONV_HALO:, :]
                acc = acc + shifted * cw[k:k + 1]
            st["y_conv"] = (st["z"][0] * acc).astype(jnp.bfloat16)

        def pool(s, st):
            v_pool = st["z"][3]
            ext = with_history(s, st, "v_tail", v_pool, carry_v, POOL_HALO)
            sums, part = [], ext
            for g, win in enumerate(POOL_WINDOWS):
                part = part + pltpu.roll(part, win // 2, axis=0)
                sums.append(part[POOL_HALO:, 0:group_dim])
                part = part[:, group_dim:]
            pos = (j * tm + s * sub + 1
                   + lax.broadcasted_iota(jnp.int32, (sub, 1), 0)).astype(jnp.float32)
            pooled = [
                sums[g] / jnp.minimum(pos, float(win))
                - v_pool[:, g * group_dim:(g + 1) * group_dim]
                for g, win in enumerate(POOL_WINDOWS)]
            st["p_in"] = jnp.concatenate(pooled, axis=1).astype(jnp.bfloat16)

        def out_proj(s, st, k):
            if k == 0:
                st["mix"] = _dot(st["y_conv"], w_out_ref[0:conv_ch, :])
            else:
                st["mix"] = st["mix"] + _dot(st["p_in"], w_comb[...])

        def norm1(s, st):
            x = x_buf[slot, pl.ds(s * sub, sub), :]
            x1 = _layer_norm(alpha * x + (1.0 + g1) * st["mix"], ln1_g_ref[...], ln1_b_ref[...])
            st["x1"] = x1
            st["h2"] = (x1 * (1.0 + sc2) + sh2).astype(jnp.bfloat16)

        def mlp_in(s, st, c):
            c0 = c * FF_CHUNK
            hid = _dot(st["h2"], w1_ref[:, c0:c0 + FF_CHUNK])
            st.setdefault("hid", []).append(
                jnp.square(jnp.maximum(hid, 0.0)).astype(jnp.bfloat16))

        def mlp_out(s, st, parts=1):
            hid = jnp.concatenate(st["hid"], axis=1)
            rows = sub // parts
            st["f"] = [_dot(hid[r:r + rows, :], w2_ref[...]) for r in range(0, sub, rows)]

        def norm2(s, st, defer_last=False):
            rows = sub // len(st["f"])
            for i, f in enumerate(st["f"]):
                x1 = st["x1"][i * rows:(i + 1) * rows]
                if defer_last and i == len(st["f"]) - 1:
                    late_f[...] = f
                    late_x1[...] = x1
                else:
                    finish(x1, f, g2, slot, late + s * sub + i * rows)

        assert n_sub == 2 and n_in in (2, 4)
        subs = [{} for _ in range(n_sub)]
        a, b = subs
        load(0, a)
        late_norm2(slot, jnp.maximum(t - 1, 0))
        for n in range(n_in):
            in_proj(0, a, n)
        load(1, b)
        in_proj(1, b, 0)
        conv(0, a)
        if n_in == 4:
            in_proj(1, b, 1)
        pool(0, a)
        for n in range(n_in // 2, n_in):
            in_proj(1, b, n)
        out_proj(0, a, 0)
        conv(1, b)
        out_proj(0, a, 1)
        pool(1, b)
        out_proj(1, b, 0)
        norm1(0, a)
        out_proj(1, b, 1)
        mlp_in(0, a, 0)
        norm1(1, b)
        for c in range(1, n_ff):
            mlp_in(0, a, c)
        mlp_out(0, a)
        mlp_in(1, b, 0)
        norm2(0, a)
        for c in range(1, n_ff):
            mlp_in(1, b, c)
        mlp_out(1, b, parts=LAST_OUT_PARTS)
        norm2(1, b, defer_last=True)

        @pl.when(t == 0)
        def _():
            write_back(slot, late, tm - late, 0).start()

        @pl.when(t > 0)
        def _():
            write_back(slot, 0, tm, t * tm - late).start()

        return 0

    lax.fori_loop(0, n_tiles, tile_body, 0)

    last = n_tiles - 1
    other = (last + 1) % IO_SLOTS
    write_back(other, 0, tm, 0).wait()
    late_norm2(other, last)
    tail = write_back(other, 0, late, n_tiles * tm - late)
    tail.start()
    tail.wait()
    write_back(last % IO_SLOTS, 0, tm, 0).wait()


def _layer(x, c, w_ada, b_ada, w_in, conv_w, w_pool, pool_scale, w_out, ln1_g, ln1_b, w1, w2,
           ln2_g, ln2_b, *, alpha, layer):
    batch, seq, d = x.shape
    conv_ch = conv_w.shape[1]
    n_groups, group_dim, _ = w_pool.shape
    pool_width = n_groups * group_dim
    tm, sub = TOKEN_TILE, SUB_TILE
    assert seq % tm == 0 and tm % sub == 0 and sub >= POOL_HALO and batch <= MOD_ROWS
    assert batch * seq // tm > IO_SLOTS
    assert max(POOL_WINDOWS) <= POOL_HALO and CONV_K - 1 <= CONV_HALO
    assert all(win == 2 ** (g + 1) for g, win in enumerate(POOL_WINDOWS))
    assert w1.shape[2] % FF_CHUNK == 0 and STAGE_ROWS % group_dim == 0
    assert COL_PIECE % conv_ch == 0 and pool_width == conv_ch and w_in.shape[2] % COL_PIECE == 0
    in_hbm = (w_ada, w_in, w_out, w1, w2)
    assert all(w.shape[1] % STAGE_ROWS == 0 and w.shape[2] % STAGE_COLS == 0 for w in in_hbm)
    bf16 = jnp.bfloat16
    row = lambda a: a.reshape(1, -1)
    x_rows = x.reshape(batch * seq, d)
    operands = (x_rows, c, w_ada, row(b_ada), w_in, conv_w, w_pool, row(pool_scale), w_out,
                row(ln1_g), row(ln1_b), w1, w2, row(ln2_g), row(ln2_b))
    in_specs = [pl.BlockSpec(memory_space=pl.ANY) if a is x_rows or any(a is w for w in in_hbm)
                else pl.BlockSpec(memory_space=pltpu.VMEM) for a in operands]
    return pl.pallas_call(
        functools.partial(_layer_kernel, alpha=alpha, sub=sub, layer=layer,
                          n_tiles=batch * seq // tm, tiles_per_seq=seq // tm),
        in_specs=in_specs,
        out_specs=pl.BlockSpec(memory_space=pl.ANY),
        out_shape=jax.ShapeDtypeStruct((batch * seq, d), x.dtype),
        scratch_shapes=[
            pltpu.VMEM((CONV_HALO, conv_ch), jnp.float32),
            pltpu.VMEM((POOL_HALO, pool_width), jnp.float32),
            pltpu.VMEM((pool_width, d), bf16),
            pltpu.VMEM((MOD_ROWS, w_ada.shape[2]), jnp.float32),
            pltpu.VMEM(w_in.shape[1:], bf16),
            pltpu.VMEM((conv_ch, d), bf16),
            pltpu.VMEM(w1.shape[1:], bf16),
            pltpu.VMEM(w2.shape[1:], bf16),
            pltpu.SemaphoreType.DMA(((2 * IO_SLOTS - 1) * (tm // STAGE_ROWS),)),
            pltpu.VMEM((IO_SLOTS, tm, d), jnp.float32),
            pltpu.SemaphoreType.DMA((IO_SLOTS,)),
            pltpu.VMEM((IO_SLOTS, tm, d), jnp.float32),
            pltpu.SemaphoreType.DMA((IO_SLOTS,)),
            pltpu.VMEM((sub // LAST_OUT_PARTS, d), jnp.float32),
            pltpu.VMEM((sub // LAST_OUT_PARTS, d), jnp.float32),
        ],
        compiler_params=pltpu.CompilerParams(vmem_limit_bytes=VMEM_LIMIT_BYTES),
        name="hybrid_layer",
    )(*operands).reshape(x.shape)


def kernel(x, c, w_ada, b_ada, w_in, conv_w, w_pool, pool_scale, w_out, ln1_g, ln1_b, w_mlp_in,
           w_mlp_out, ln2_g, ln2_b):
    depth = w_in.shape[0]
    alpha = (2.0 * depth) ** 0.25
    for l in range(depth):
        x = _layer(x, c, w_ada, b_ada[l], w_in, conv_w[l], w_pool[l], pool_scale[l], w_out,
                   ln1_g[l], ln1_b[l], w_mlp_in, w_mlp_out, ln2_g[l], ln2_b[l], alpha=alpha,
                   layer=l)
    return x
```

```python
import functools

import jax
import jax.numpy as jnp
from jax import lax
from jax.experimental import pallas as pl
from jax.experimental.pallas import tpu as pltpu

CONV_K = 3
POOL_WINDOWS = (2, 4, 8, 16)
N_MOD = 6
LN_EPS = 1e-5

CONV_HALO = 8
POOL_HALO = 16
MOD_ROWS = 8

TOKEN_TILE = 1024
SUB_TILE = 512
COL_PIECE = 1024
FF_CHUNK = 1024
LAST_OUT_PARTS = 2
IO_SLOTS = 2
STAGE_SLOTS = 3
STAGE_ROWS = 512
STAGE_COLS = 1024
VMEM_LIMIT_BYTES = 60 * 1024 * 1024


def _layer_norm(r, g, b):
    mu = jnp.mean(r, axis=-1, keepdims=True)
    d = r - mu
    var = jnp.mean(d * d, axis=-1, keepdims=True)
    return d * lax.rsqrt(var + LN_EPS) * g + b


def _dot(a, b):
    return jnp.dot(a, b, preferred_element_type=jnp.float32)


def _layer_kernel(x_hbm, c_ref, w_ada_hbm, b_ada_ref, w_in_hbm, conv_w_ref, w_pool_ref,
                  pool_scale_ref, w_out_hbm, ln1_g_ref, ln1_b_ref, w1_hbm, w2_hbm, ln2_g_ref,
                  ln2_b_ref, o_hbm,
                  carry_u, carry_v, w_comb, mod_s, w_in_ref, w_out_ref, w1_ref, w2_ref, stage,
                  stage_sem, x_buf, x_sem, out_buf, out_sem, late_f, late_x1,
                  *, alpha, sub, layer, n_tiles, tiles_per_seq):
    tm, d = x_buf.shape[1], x_buf.shape[2]
    n_sub = tm // sub
    conv_ch = conv_w_ref.shape[1]
    n_groups, group_dim, _ = w_pool_ref.shape
    d_ff = w1_ref.shape[1]
    late = late_f.shape[0]
    n_in = w_in_ref.shape[1] // COL_PIECE
    n_ff = d_ff // FF_CHUNK

    def x_fetch(t):
        return pltpu.make_async_copy(x_hbm.at[pl.ds(t * tm, tm), :], x_buf.at[t % IO_SLOTS],
                                     x_sem.at[t % IO_SLOTS])

    def write_back(slot, buf_row0, rows, hbm_row0):
        return pltpu.make_async_copy(out_buf.at[slot, pl.ds(buf_row0, rows), :],
                                     o_hbm.at[pl.ds(hbm_row0, rows), :], out_sem.at[slot])

    x_fetch(0).start()
    late_f[...] = jnp.zeros(late_f.shape, late_f.dtype)
    late_x1[...] = jnp.zeros(late_x1.shape, late_x1.dtype)

    c = c_ref[...]
    cond = c * jax.nn.sigmoid(c)
    cond = jnp.concatenate(
        [cond, jnp.zeros((MOD_ROWS - c.shape[0], d), cond.dtype)], axis=0).astype(jnp.bfloat16)

    def into_mod(r0, c0, block):
        part = _dot(cond[:, r0:r0 + STAGE_ROWS], block.astype(jnp.bfloat16))
        base = b_ada_ref[:, c0:c0 + STAGE_COLS] if r0 == 0 else mod_s[:, c0:c0 + STAGE_COLS]
        mod_s[:, c0:c0 + STAGE_COLS] = base + part

    def into(dst):
        def keep(r0, c0, block):
            dst[r0:r0 + STAGE_ROWS, c0:c0 + STAGE_COLS] = block.astype(dst.dtype)
        return keep

    def into_out_proj(r0, c0, block):
        if r0 < conv_ch:
            into(w_out_ref)(r0, c0, block)
        for g in range(n_groups):
            g0 = conv_ch + g * group_dim
            if r0 <= g0 and g0 + group_dim <= r0 + STAGE_ROWS:
                scaled = w_pool_ref[g] * pool_scale_ref[:, g * group_dim:(g + 1) * group_dim]
                w_comb[g * group_dim:(g + 1) * group_dim, c0:c0 + STAGE_COLS] = jnp.dot(
                    scaled, block[g0 - r0:g0 - r0 + group_dim, :],
                    precision=lax.Precision.HIGHEST,
                    preferred_element_type=jnp.float32).astype(w_comb.dtype)

    blocks = [(hbm, r0, c0, consume)
              for hbm, shape, consume in (
                  (w_ada_hbm, (d, mod_s.shape[1]), into_mod),
                  (w_in_hbm, w_in_ref.shape, into(w_in_ref)),
                  (w_out_hbm, (d, d), into_out_proj),
                  (w1_hbm, w1_ref.shape, into(w1_ref)),
                  (w2_hbm, w2_ref.shape, into(w2_ref)))
              for r0 in range(0, shape[0], STAGE_ROWS)
              for c0 in range(0, shape[1], STAGE_COLS)]

    assert tm % STAGE_ROWS == 0 and d == STAGE_COLS
    slots = [stage.at[k] for k in range(STAGE_SLOTS)]
    slots += [out_buf.at[k, pl.ds(r, STAGE_ROWS), :]
              for k in range(IO_SLOTS) for r in range(0, tm, STAGE_ROWS)]

    def fetch(i):
        hbm, r0, c0, _ = blocks[i]
        return pltpu.make_async_copy(
            hbm.at[layer, pl.ds(r0, STAGE_ROWS), pl.ds(c0, STAGE_COLS)],
            slots[i % len(slots)], stage_sem.at[i % len(slots)])

    ahead = len(slots) - 1
    for i in range(min(ahead, len(blocks))):
        fetch(i).start()
    for i, (_, r0, c0, consume) in enumerate(blocks):
        if i + ahead < len(blocks):
            fetch(i + ahead).start()
        fetch(i).wait()
        consume(r0, c0, slots[i % len(slots)][...])

    cw = conv_w_ref[...]

    def finish(x1, f, gate, slot, buf_row0):
        out_buf[slot, pl.ds(buf_row0, f.shape[0]), :] = _layer_norm(
            alpha * x1 + (1.0 + gate) * f, ln2_g_ref[...], ln2_b_ref[...])

    def late_norm2(slot, owner_tile):
        gate = mod_s[pl.ds(owner_tile // tiles_per_seq, 1), (N_MOD - 1) * d:]
        finish(late_x1[...], late_f[...], gate, slot, 0)

    def tile_body(t, _):
        slot = t % IO_SLOTS
        j = t % tiles_per_seq

        x_fetch(t).wait()

        @pl.when(t + 1 < n_tiles)
        def _():
            x_fetch(t + 1).start()

        @pl.when(t == IO_SLOTS)
        def _():
            write_back(slot, late, tm - late, 0).wait()

        @pl.when(t > IO_SLOTS)
        def _():
            write_back(slot, 0, tm, 0).wait()

        @pl.when(j == 0)
        def _():
            carry_u[...] = jnp.zeros(carry_u.shape, carry_u.dtype)
            carry_v[...] = jnp.zeros(carry_v.shape, carry_v.dtype)

        mod = mod_s[pl.ds(t // tiles_per_seq, 1), :]
        sh1, sc1, g1, sh2, sc2, g2 = (mod[:, i * d:(i + 1) * d] for i in range(N_MOD))

        def load(s, st):
            x = x_buf[slot, pl.ds(s * sub, sub), :]
            st["h"] = (x * (1.0 + sc1) + sh1).astype(jnp.bfloat16)
            st["z"] = {}

        def in_proj(s, st, n):
            piece = _dot(st["h"], w_in_ref[:, n * COL_PIECE:(n + 1) * COL_PIECE])
            for k in range(COL_PIECE // conv_ch):
                st["z"][n * (COL_PIECE // conv_ch) + k] = piece[:, k * conv_ch:(k + 1) * conv_ch]

        def with_history(s, st, key, cur, carry, halo):
            st[key] = cur[sub - halo:, :]
            if s == n_sub - 1:
                carry[...] = st[key]
            return jnp.concatenate([carry[...] if s == 0 else subs[s - 1][key], cur], axis=0)

        def conv(s, st):
            u = st["z"][1] * st["z"][2]
            ext = with_history(s, st, "u_tail", u, carry_u, CONV_HALO)
            acc = u * cw[CONV_K - 1:CONV_K]
            for k in range(CONV_K - 1):
                shifted = pltpu.roll(ext, CONV_K - 1 - k, axis=0)[CONV_HALO:, :]
                acc = acc + shifted * cw[k:k + 1]
            st["y_conv"] = (st["z"][0] * acc).astype(jnp.bfloat16)

        def pool(s, st):
            v_pool = st["z"][3]
            ext = with_history(s, st, "v_tail", v_pool, carry_v, POOL_HALO)
            sums, part = [], ext
            for g, win in enumerate(POOL_WINDOWS):
                part = part + pltpu.roll(part, win // 2, axis=0)
                sums.append(part[POOL_HALO:, 0:group_dim])
                part = part[:, group_dim:]
            pos = (j * tm + s * sub + 1
                   + lax.broadcasted_iota(jnp.int32, (sub, 1), 0)).astype(jnp.float32)
            pooled = []
            for g, win in enumerate(POOL_WINDOWS):
                assert win & (win - 1) == 0 and win <= POOL_HALO <= sub
                mean = sums[g] * (1.0 / win)
                if s == 0:
                    head = sums[g][:POOL_HALO] / jnp.minimum(pos[:POOL_HALO], float(win))
                    mean = jnp.concatenate(
                        [jnp.where(j == 0, head, mean[:POOL_HALO]), mean[POOL_HALO:]], axis=0)
                pooled.append(mean - v_pool[:, g * group_dim:(g + 1) * group_dim])
            st["p_in"] = jnp.concatenate(pooled, axis=1).astype(jnp.bfloat16)

        def out_proj(s, st, k):
            if k == 0:
                st["mix"] = _dot(st["y_conv"], w_out_ref[0:conv_ch, :])
            else:
                st["mix"] = st["mix"] + _dot(st["p_in"], w_comb[...])

        def norm1(s, st):
            x = x_buf[slot, pl.ds(s * sub, sub), :]
            x1 = _layer_norm(alpha * x + (1.0 + g1) * st["mix"], ln1_g_ref[...], ln1_b_ref[...])
            st["x1"] = x1
            st["h2"] = (x1 * (1.0 + sc2) + sh2).astype(jnp.bfloat16)

        def mlp_in(s, st, c):
            c0 = c * FF_CHUNK
            hid = _dot(st["h2"], w1_ref[:, c0:c0 + FF_CHUNK])
            st.setdefault("hid", []).append(
                jnp.square(jnp.maximum(hid, 0.0)).astype(jnp.bfloat16))

        def mlp_out(s, st, parts=1):
            hid = jnp.concatenate(st["hid"], axis=1)
            rows = sub // parts
            st["f"] = [_dot(hid[r:r + rows, :], w2_ref[...]) for r in range(0, sub, rows)]

        def norm2(s, st, defer_last=False):
            rows = sub // len(st["f"])
            for i, f in enumerate(st["f"]):
                x1 = st["x1"][i * rows:(i + 1) * rows]
                if defer_last and i == len(st["f"]) - 1:
                    late_f[...] = f
                    late_x1[...] = x1
                else:
                    finish(x1, f, g2, slot, late + s * sub + i * rows)

        assert n_sub == 2 and n_in in (2, 4)
        subs = [{} for _ in range(n_sub)]
        a, b = subs
        load(0, a)
        late_norm2(slot, jnp.maximum(t - 1, 0))
        for n in range(n_in):
            in_proj(0, a, n)
        load(1, b)
        in_proj(1, b, 0)
        conv(0, a)
        if n_in == 4:
            in_proj(1, b, 1)
        pool(0, a)
        for n in range(n_in // 2, n_in):
            in_proj(1, b, n)
        out_proj(0, a, 0)
        conv(1, b)
        out_proj(0, a, 1)
        pool(1, b)
        out_proj(1, b, 0)
        norm1(0, a)
        out_proj(1, b, 1)
        mlp_in(0, a, 0)
        norm1(1, b)
        for c in range(1, n_ff):
            mlp_in(0, a, c)
        mlp_out(0, a)
        mlp_in(1, b, 0)
        norm2(0, a)
        for c in range(1, n_ff):
            mlp_in(1, b, c)
        mlp_out(1, b, parts=LAST_OUT_PARTS)
        norm2(1, b, defer_last=True)

        @pl.when(t == 0)
        def _():
            write_back(slot, late, tm - late, 0).start()

        @pl.when(t > 0)
        def _():
            write_back(slot, 0, tm, t * tm - late).start()

        return 0

    lax.fori_loop(0, n_tiles, tile_body, 0)

    last = n_tiles - 1
    other = (last + 1) % IO_SLOTS
    write_back(other, 0, tm, 0).wait()
    late_norm2(other, last)
    tail = write_back(other, 0, late, n_tiles * tm - late)
    tail.start()
    tail.wait()
    write_back(last % IO_SLOTS, 0, tm, 0).wait()


def _layer(x, c, w_ada, b_ada, w_in, conv_w, w_pool, pool_scale, w_out, ln1_g, ln1_b, w1, w2,
           ln2_g, ln2_b, *, alpha, layer):
    batch, seq, d = x.shape
    conv_ch = conv_w.shape[1]
    n_groups, group_dim, _ = w_pool.shape
    pool_width = n_groups * group_dim
    tm, sub = TOKEN_TILE, SUB_TILE
    assert seq % tm == 0 and tm % sub == 0 and sub >= POOL_HALO and batch <= MOD_ROWS
    assert batch * seq // tm > IO_SLOTS
    assert max(POOL_WINDOWS) <= POOL_HALO and CONV_K - 1 <= CONV_HALO
    assert all(win == 2 ** (g + 1) for g, win in enumerate(POOL_WINDOWS))
    assert w1.shape[2] % FF_CHUNK == 0 and STAGE_ROWS % group_dim == 0
    assert COL_PIECE % conv_ch == 0 and pool_width == conv_ch and w_in.shape[2] % COL_PIECE == 0
    in_hbm = (w_ada, w_in, w_out, w1, w2)
    assert all(w.shape[1] % STAGE_ROWS == 0 and w.shape[2] % STAGE_COLS == 0 for w in in_hbm)
    bf16 = jnp.bfloat16
    row = lambda a: a.reshape(1, -1)
    x_rows = x.reshape(batch * seq, d)
    operands = (x_rows, c, w_ada, row(b_ada), w_in, conv_w, w_pool, row(pool_scale), w_out,
                row(ln1_g), row(ln1_b), w1, w2, row(ln2_g), row(ln2_b))
    in_specs = [pl.BlockSpec(memory_space=pl.ANY) if a is x_rows or any(a is w for w in in_hbm)
                else pl.BlockSpec(memory_space=pltpu.VMEM) for a in operands]
    return pl.pallas_call(
        functools.partial(_layer_kernel, alpha=alpha, sub=sub, layer=layer,
                          n_tiles=batch * seq // tm, tiles_per_seq=seq // tm),
        in_specs=in_specs,
        out_specs=pl.BlockSpec(memory_space=pl.ANY),
        out_shape=jax.ShapeDtypeStruct((batch * seq, d), x.dtype),
        scratch_shapes=[
            pltpu.VMEM((CONV_HALO, conv_ch), jnp.float32),
            pltpu.VMEM((POOL_HALO, pool_width), jnp.float32),
            pltpu.VMEM((pool_width, d), bf16),
            pltpu.VMEM((MOD_ROWS, w_ada.shape[2]), jnp.float32),
            pltpu.VMEM(w_in.shape[1:], bf16),
            pltpu.VMEM((conv_ch, d), bf16),
            pltpu.VMEM(w1.shape[1:], bf16),
            pltpu.VMEM(w2.shape[1:], bf16),
            pltpu.VMEM((STAGE_SLOTS, STAGE_ROWS, STAGE_COLS), jnp.float32),
            pltpu.SemaphoreType.DMA((STAGE_SLOTS + IO_SLOTS * (tm // STAGE_ROWS),)),
            pltpu.VMEM((IO_SLOTS, tm, d), jnp.float32),
            pltpu.SemaphoreType.DMA((IO_SLOTS,)),
            pltpu.VMEM((IO_SLOTS, tm, d), jnp.float32),
            pltpu.SemaphoreType.DMA((IO_SLOTS,)),
            pltpu.VMEM((sub // LAST_OUT_PARTS, d), jnp.float32),
            pltpu.VMEM((sub // LAST_OUT_PARTS, d), jnp.float32),
        ],
        compiler_params=pltpu.CompilerParams(vmem_limit_bytes=VMEM_LIMIT_BYTES),
        name="hybrid_layer",
    )(*operands).reshape(x.shape)


def kernel(x, c, w_ada, b_ada, w_in, conv_w, w_pool, pool_scale, w_out, ln1_g, ln1_b, w_mlp_in,
           w_mlp_out, ln2_g, ln2_b):
    depth = w_in.shape[0]
    alpha = (2.0 * depth) ** 0.25
    for l in range(depth):
        x = _layer(x, c, w_ada, b_ada[l], w_in, conv_w[l], w_pool[l], pool_scale[l], w_out,
                   ln1_g[l], ln1_b[l], w_mlp_in, w_mlp_out, ln2_g[l], ln2_b[l], alpha=alpha,
                   layer=l)
    return x
```

```python
import functools

import jax
import jax.numpy as jnp
from jax import lax
from jax.experimental import pallas as pl
from jax.experimental.pallas import tpu as pltpu

CONV_K = 3
POOL_WINDOWS = (2, 4, 8, 16)
N_MOD = 6
LN_EPS = 1e-5

CONV_HALO = 8
POOL_HALO = 16
MOD_ROWS = 8

TOKEN_TILE = 1024
SUB_TILE = 512
COL_PIECE = 1024
FF_CHUNK = 1024
LAST_OUT_PARTS = 2
IO_SLOTS = 2
STAGE_SLOTS = 3
STAGE_ROWS = 512
STAGE_COLS = 1024
VMEM_LIMIT_BYTES = 60 * 1024 * 1024


def _row_mean(v):
    n = v.shape[-1]
    total = jnp.sum(v, axis=-1, keepdims=True)
    return total * (1.0 / n) if n & (n - 1) == 0 else total / n


def _layer_norm(r, g, b):
    mu = _row_mean(r)
    d = r - mu
    var = _row_mean(d * d)
    return d * lax.rsqrt(var + LN_EPS) * g + b


def _dot(a, b):
    return jnp.dot(a, b, preferred_element_type=jnp.float32)


def _layer_kernel(x_hbm, c_ref, w_ada_hbm, b_ada_ref, w_in_hbm, conv_w_ref, w_pool_ref,
                  pool_scale_ref, w_out_hbm, ln1_g_ref, ln1_b_ref, w1_hbm, w2_hbm, ln2_g_ref,
                  ln2_b_ref, o_hbm,
                  carry_u, carry_v, w_comb, mod_s, w_in_ref, w_out_ref, w1_ref, w2_ref, stage,
                  stage_sem, x_buf, x_sem, out_buf, out_sem, late_f, late_x1,
                  *, alpha, sub, layer, n_tiles, tiles_per_seq):
    tm, d = x_buf.shape[1], x_buf.shape[2]
    n_sub = tm // sub
    conv_ch = conv_w_ref.shape[1]
    n_groups, group_dim, _ = w_pool_ref.shape
    d_ff = w1_ref.shape[1]
    late = late_f.shape[0]
    n_in = w_in_ref.shape[1] // COL_PIECE
    n_ff = d_ff // FF_CHUNK

    def x_fetch(t):
        return pltpu.make_async_copy(x_hbm.at[pl.ds(t * tm, tm), :], x_buf.at[t % IO_SLOTS],
                                     x_sem.at[t % IO_SLOTS])

    def write_back(slot, buf_row0, rows, hbm_row0):
        return pltpu.make_async_copy(out_buf.at[slot, pl.ds(buf_row0, rows), :],
                                     o_hbm.at[pl.ds(hbm_row0, rows), :], out_sem.at[slot])

    x_fetch(0).start()
    late_f[...] = jnp.zeros(late_f.shape, late_f.dtype)
    late_x1[...] = jnp.zeros(late_x1.shape, late_x1.dtype)

    c = c_ref[...]
    cond = c * jax.nn.sigmoid(c)
    cond = jnp.concatenate(
        [cond, jnp.zeros((MOD_ROWS - c.shape[0], d), cond.dtype)], axis=0).astype(jnp.bfloat16)

    def into_mod(r0, c0, block):
        part = _dot(cond[:, r0:r0 + STAGE_ROWS], block.astype(jnp.bfloat16))
        base = b_ada_ref[:, c0:c0 + STAGE_COLS] if r0 == 0 else mod_s[:, c0:c0 + STAGE_COLS]
        mod_s[:, c0:c0 + STAGE_COLS] = base + part

    def into(dst):
        def keep(r0, c0, block):
            dst[r0:r0 + STAGE_ROWS, c0:c0 + STAGE_COLS] = block.astype(dst.dtype)
        return keep

    def into_out_proj(r0, c0, block):
        if r0 < conv_ch:
            into(w_out_ref)(r0, c0, block)
        for g in range(n_groups):
            g0 = conv_ch + g * group_dim
            if r0 <= g0 and g0 + group_dim <= r0 + STAGE_ROWS:
                scaled = w_pool_ref[g] * pool_scale_ref[:, g * group_dim:(g + 1) * group_dim]
                w_comb[g * group_dim:(g + 1) * group_dim, c0:c0 + STAGE_COLS] = jnp.dot(
                    scaled, block[g0 - r0:g0 - r0 + group_dim, :],
                    precision=lax.Precision.HIGHEST,
                    preferred_element_type=jnp.float32).astype(w_comb.dtype)

    blocks = [(hbm, r0, c0, consume)
              for hbm, shape, consume in (
                  (w_ada_hbm, (d, mod_s.shape[1]), into_mod),
                  (w_in_hbm, w_in_ref.shape, into(w_in_ref)),
                  (w_out_hbm, (d, d), into_out_proj),
                  (w1_hbm, w1_ref.shape, into(w1_ref)),
                  (w2_hbm, w2_ref.shape, into(w2_ref)))
              for r0 in range(0, shape[0], STAGE_ROWS)
              for c0 in range(0, shape[1], STAGE_COLS)]

    assert tm % STAGE_ROWS == 0 and d == STAGE_COLS
    slots = [stage.at[k] for k in range(STAGE_SLOTS)]
    slots += [out_buf.at[k, pl.ds(r, STAGE_ROWS), :]
              for k in range(IO_SLOTS) for r in range(0, tm, STAGE_ROWS)]

    def fetch(i):
        hbm, r0, c0, _ = blocks[i]
        return pltpu.make_async_copy(
            hbm.at[layer, pl.ds(r0, STAGE_ROWS), pl.ds(c0, STAGE_COLS)],
            slots[i % len(slots)], stage_sem.at[i % len(slots)])

    ahead = len(slots) - 1
    for i in range(min(ahead, len(blocks))):
        fetch(i).start()
    for i, (_, r0, c0, consume) in enumerate(blocks):
        if i + ahead < len(blocks):
            fetch(i + ahead).start()
        fetch(i).wait()
        consume(r0, c0, slots[i % len(slots)][...])

    cw = conv_w_ref[...]

    def finish(x1, f, gate, slot, buf_row0):
        out_buf[slot, pl.ds(buf_row0, f.shape[0]), :] = _layer_norm(
            alpha * x1 + (1.0 + gate) * f, ln2_g_ref[...], ln2_b_ref[...])

    def late_norm2(slot, owner_tile):
        gate = mod_s[pl.ds(owner_tile // tiles_per_seq, 1), (N_MOD - 1) * d:]
        finish(late_x1[...], late_f[...], gate, slot, 0)

    def tile_body(t, _):
        slot = t % IO_SLOTS
        j = t % tiles_per_seq

        x_fetch(t).wait()

        @pl.when(t + 1 < n_tiles)
        def _():
            x_fetch(t + 1).start()

        @pl.when(t == IO_SLOTS)
        def _():
            write_back(slot, late, tm - late, 0).wait()

        @pl.when(t > IO_SLOTS)
        def _():
            write_back(slot, 0, tm, 0).wait()

        @pl.when(j == 0)
        def _():
            carry_u[...] = jnp.zeros(carry_u.shape, carry_u.dtype)
            carry_v[...] = jnp.zeros(carry_v.shape, carry_v.dtype)

        mod = mod_s[pl.ds(t // tiles_per_seq, 1), :]
        sh1, sc1, g1, sh2, sc2, g2 = (mod[:, i * d:(i + 1) * d] for i in range(N_MOD))

        def load(s, st):
            x = x_buf[slot, pl.ds(s * sub, sub), :]
            st["h"] = (x * (1.0 + sc1) + sh1).astype(jnp.bfloat16)
            st["z"] = {}

        def in_proj(s, st, n):
            piece = _dot(st["h"], w_in_ref[:, n * COL_PIECE:(n + 1) * COL_PIECE])
            for k in range(COL_PIECE // conv_ch):
                st["z"][n * (COL_PIECE // conv_ch) + k] = piece[:, k * conv_ch:(k + 1) * conv_ch]

        def with_history(s, st, key, cur, carry, halo):
            st[key] = cur[sub - halo:, :]
            if s == n_sub - 1:
                carry[...] = st[key]
            return jnp.concatenate([carry[...] if s == 0 else subs[s - 1][key], cur], axis=0)

        def conv(s, st):
            u = st["z"][1] * st["z"][2]
            ext = with_history(s, st, "u_tail", u, carry_u, CONV_HALO)
            acc = u * cw[CONV_K - 1:CONV_K]
            for k in range(CONV_K - 1):
                shifted = pltpu.roll(ext, CONV_K - 1 - k, axis=0)[CONV_HALO:, :]
                acc = acc + shifted * cw[k:k + 1]
            st["y_conv"] = (st["z"][0] * acc).astype(jnp.bfloat16)

        def pool(s, st):
            v_pool = st["z"][3]
            ext = with_history(s, st, "v_tail", v_pool, carry_v, POOL_HALO)
            sums, part = [], ext
            for g, win in enumerate(POOL_WINDOWS):
                part = part + pltpu.roll(part, win // 2, axis=0)
                sums.append(part[POOL_HALO:, 0:group_dim])
                part = part[:, group_dim:]
            pos = (j * tm + s * sub + 1
                   + lax.broadcasted_iota(jnp.int32, (sub, 1), 0)).astype(jnp.float32)
            pooled = []
            for g, win in enumerate(POOL_WINDOWS):
                assert win & (win - 1) == 0 and win <= POOL_HALO <= sub
                mean = sums[g] * (1.0 / win)
                if s == 0:
                    head = sums[g][:POOL_HALO] / jnp.minimum(pos[:POOL_HALO], float(win))
                    mean = jnp.concatenate(
                        [jnp.where(j == 0, head, mean[:POOL_HALO]), mean[POOL_HALO:]], axis=0)
                pooled.append(mean - v_pool[:, g * group_dim:(g + 1) * group_dim])
            st["p_in"] = jnp.concatenate(pooled, axis=1).astype(jnp.bfloat16)

        def out_proj(s, st, k):
            if k == 0:
                st["mix"] = _dot(st["y_conv"], w_out_ref[0:conv_ch, :])
            else:
                st["mix"] = st["mix"] + _dot(st["p_in"], w_comb[...])

        def norm1(s, st):
            x = x_buf[slot, pl.ds(s * sub, sub), :]
            x1 = _layer_norm(alpha * x + (1.0 + g1) * st["mix"], ln1_g_ref[...], ln1_b_ref[...])
            st["x1"] = x1
            st["h2"] = (x1 * (1.0 + sc2) + sh2).astype(jnp.bfloat16)

        def mlp_in(s, st, c):
            c0 = c * FF_CHUNK
            hid = _dot(st["h2"], w1_ref[:, c0:c0 + FF_CHUNK])
            st.setdefault("hid", []).append(
                jnp.square(jnp.maximum(hid, 0.0)).astype(jnp.bfloat16))

        def mlp_out(s, st, parts=1):
            hid = jnp.concatenate(st["hid"], axis=1)
            rows = sub // parts
            st["f"] = [_dot(hid[r:r + rows, :], w2_ref[...]) for r in range(0, sub, rows)]

        def norm2(s, st, defer_last=False):
            rows = sub // len(st["f"])
            for i, f in enumerate(st["f"]):
                x1 = st["x1"][i * rows:(i + 1) * rows]
                if defer_last and i == len(st["f"]) - 1:
                    late_f[...] = f
                    late_x1[...] = x1
                else:
                    finish(x1, f, g2, slot, late + s * sub + i * rows)

        assert n_sub == 2 and n_in in (2, 4)
        subs = [{} for _ in range(n_sub)]
        a, b = subs
        load(0, a)
        late_norm2(slot, jnp.maximum(t - 1, 0))
        for n in range(n_in):
            in_proj(0, a, n)
        load(1, b)
        in_proj(1, b, 0)
        conv(0, a)
        if n_in == 4:
            in_proj(1, b, 1)
        pool(0, a)
        for n in range(n_in // 2, n_in):
            in_proj(1, b, n)
        out_proj(0, a, 0)
        conv(1, b)
        out_proj(0, a, 1)
        pool(1, b)
        out_proj(1, b, 0)
        norm1(0, a)
        out_proj(1, b, 1)
        mlp_in(0, a, 0)
        norm1(1, b)
        for c in range(1, n_ff):
            mlp_in(0, a, c)
        mlp_out(0, a)
        mlp_in(1, b, 0)
        norm2(0, a)
        for c in range(1, n_ff):
            mlp_in(1, b, c)
        mlp_out(1, b, parts=LAST_OUT_PARTS)
        norm2(1, b, defer_last=True)

        @pl.when(t == 0)
        def _():
            write_back(slot, late, tm - late, 0).start()

        @pl.when(t > 0)
        def _():
            write_back(slot, 0, tm, t * tm - late).start()

        return 0

    lax.fori_loop(0, n_tiles, tile_body, 0)

    last = n_tiles - 1
    other = (last + 1) % IO_SLOTS
    write_back(other, 0, tm, 0).wait()
    late_norm2(other, last)
    tail = write_back(other, 0, late, n_tiles * tm - late)
    tail.start()
    tail.wait()
    write_back(last % IO_SLOTS, 0, tm, 0).wait()


def _layer(x, c, w_ada, b_ada, w_in, conv_w, w_pool, pool_scale, w_out, ln1_g, ln1_b, w1, w2,
           ln2_g, ln2_b, *, alpha, layer):
    batch, seq, d = x.shape
    conv_ch = conv_w.shape[1]
    n_groups, group_dim, _ = w_pool.shape
    pool_width = n_groups * group_dim
    tm, sub = TOKEN_TILE, SUB_TILE
    assert seq % tm == 0 and tm % sub == 0 and sub >= POOL_HALO and batch <= MOD_ROWS
    assert batch * seq // tm > IO_SLOTS
    assert max(POOL_WINDOWS) <= POOL_HALO and CONV_K - 1 <= CONV_HALO
    assert all(win == 2 ** (g + 1) for g, win in enumerate(POOL_WINDOWS))
    assert w1.shape[2] % FF_CHUNK == 0 and STAGE_ROWS % group_dim == 0
    assert COL_PIECE % conv_ch == 0 and pool_width == conv_ch and w_in.shape[2] % COL_PIECE == 0
    in_hbm = (w_ada, w_in, w_out, w1, w2)
    assert all(w.shape[1] % STAGE_ROWS == 0 and w.shape[2] % STAGE_COLS == 0 for w in in_hbm)
    bf16 = jnp.bfloat16
    row = lambda a: a.reshape(1, -1)
    x_rows = x.reshape(batch * seq, d)
    operands = (x_rows, c, w_ada, row(b_ada), w_in, conv_w, w_pool, row(pool_scale), w_out,
                row(ln1_g), row(ln1_b), w1, w2, row(ln2_g), row(ln2_b))
    in_specs = [pl.BlockSpec(memory_space=pl.ANY) if a is x_rows or any(a is w for w in in_hbm)
                else pl.BlockSpec(memory_space=pltpu.VMEM) for a in operands]
    return pl.pallas_call(
        functools.partial(_layer_kernel, alpha=alpha, sub=sub, layer=layer,
                          n_tiles=batch * seq // tm, tiles_per_seq=seq // tm),
        in_specs=in_specs,
        out_specs=pl.BlockSpec(memory_space=pl.ANY),
        out_shape=jax.ShapeDtypeStruct((batch * seq, d), x.dtype),
        scratch_shapes=[
            pltpu.VMEM((CONV_HALO, conv_ch), jnp.float32),
            pltpu.VMEM((POOL_HALO, pool_width), jnp.float32),
            pltpu.VMEM((pool_width, d), bf16),
            pltpu.VMEM((MOD_ROWS, w_ada.shape[2]), jnp.float32),
            pltpu.VMEM(w_in.shape[1:], bf16),
            pltpu.VMEM((conv_ch, d), bf16),
            pltpu.VMEM(w1.shape[1:], bf16),
            pltpu.VMEM(w2.shape[1:], bf16),
            pltpu.VMEM((STAGE_SLOTS, STAGE_ROWS, STAGE_COLS), jnp.float32),
            pltpu.SemaphoreType.DMA((STAGE_SLOTS + IO_SLOTS * (tm // STAGE_ROWS),)),
            pltpu.VMEM((IO_SLOTS, tm, d), jnp.float32),
            pltpu.SemaphoreType.DMA((IO_SLOTS,)),
            pltpu.VMEM((IO_SLOTS, tm, d), jnp.float32),
            pltpu.SemaphoreType.DMA((IO_SLOTS,)),
            pltpu.VMEM((sub // LAST_OUT_PARTS, d), jnp.float32),
            pltpu.VMEM((sub // LAST_OUT_PARTS, d), jnp.float32),
        ],
        compiler_params=pltpu.CompilerParams(vmem_limit_bytes=VMEM_LIMIT_BYTES),
        name="hybrid_layer",
    )(*operands).reshape(x.shape)


def kernel(x, c, w_ada, b_ada, w_in, conv_w, w_pool, pool_scale, w_out, ln1_g, ln1_b, w_mlp_in,
           w_mlp_out, ln2_g, ln2_b):
    depth = w_in.shape[0]
    alpha = (2.0 * depth) ** 0.25
    for l in range(depth):
        x = _layer(x, c, w_ada, b_ada[l], w_in, conv_w[l], w_pool[l], pool_scale[l], w_out,
                   ln1_g[l], ln1_b[l], w_mlp_in, w_mlp_out, ln2_g[l], ln2_b[l], alpha=alpha,
                   layer=l)
    return x
```

```python
import functools

import jax
import jax.numpy as jnp
from jax import lax
from jax.experimental import pallas as pl
from jax.experimental.pallas import tpu as pltpu

CONV_K = 3
POOL_WINDOWS = (2, 4, 8, 16)
N_MOD = 6
LN_EPS = 1e-5

CONV_HALO = 8
POOL_HALO = 16
MOD_ROWS = 8

TOKEN_TILE = 1024
SUB_TILE = 512
COL_PIECE = 1024
FF_CHUNK = 512
LAST_OUT_PARTS = 2
IO_SLOTS = 2
STAGE_SLOTS = 3
STAGE_ROWS = 512
STAGE_COLS = 1024
VMEM_LIMIT_BYTES = 60 * 1024 * 1024


def _layer_norm(r, g, b):
    mu = jnp.mean(r, axis=-1, keepdims=True)
    d = r - mu
    var = jnp.mean(d * d, axis=-1, keepdims=True)
    return d * lax.rsqrt(var + LN_EPS) * g + b


def _dot(a, b):
    return jnp.dot(a, b, preferred_element_type=jnp.float32)


def _layer_kernel(x_hbm, c_ref, w_ada_hbm, b_ada_ref, w_in_hbm, conv_w_ref, w_pool_ref,
                  pool_scale_ref, w_out_hbm, ln1_g_ref, ln1_b_ref, w1_hbm, w2_hbm, ln2_g_ref,
                  ln2_b_ref, o_hbm,
                  carry_u, carry_v, w_comb, mod_s, w_in_ref, w_out_ref, w1_ref, w2_ref, stage,
                  stage_sem, x_buf, x_sem, out_buf, out_sem, late_f, late_x1,
                  *, alpha, sub, layer, n_tiles, tiles_per_seq):
    tm, d = x_buf.shape[1], x_buf.shape[2]
    n_sub = tm // sub
    conv_ch = conv_w_ref.shape[1]
    n_groups, group_dim, _ = w_pool_ref.shape
    d_ff = w1_ref.shape[1]
    late = late_f.shape[0]
    n_in = w_in_ref.shape[1] // COL_PIECE
    n_ff = d_ff // FF_CHUNK

    def x_fetch(t):
        return pltpu.make_async_copy(x_hbm.at[pl.ds(t * tm, tm), :], x_buf.at[t % IO_SLOTS],
                                     x_sem.at[t % IO_SLOTS])

    def write_back(slot, buf_row0, rows, hbm_row0):
        return pltpu.make_async_copy(out_buf.at[slot, pl.ds(buf_row0, rows), :],
                                     o_hbm.at[pl.ds(hbm_row0, rows), :], out_sem.at[slot])

    x_fetch(0).start()
    late_f[...] = jnp.zeros(late_f.shape, late_f.dtype)
    late_x1[...] = jnp.zeros(late_x1.shape, late_x1.dtype)

    c = c_ref[...]
    cond = c * jax.nn.sigmoid(c)
    cond = jnp.concatenate(
        [cond, jnp.zeros((MOD_ROWS - c.shape[0], d), cond.dtype)], axis=0).astype(jnp.bfloat16)

    def into_mod(r0, c0, block):
        part = _dot(cond[:, r0:r0 + STAGE_ROWS], block.astype(jnp.bfloat16))
        base = b_ada_ref[:, c0:c0 + STAGE_COLS] if r0 == 0 else mod_s[:, c0:c0 + STAGE_COLS]
        mod_s[:, c0:c0 + STAGE_COLS] = base + part

    def into(dst):
        def keep(r0, c0, block):
            dst[r0:r0 + STAGE_ROWS, c0:c0 + STAGE_COLS] = block.astype(dst.dtype)
        return keep

    def into_out_proj(r0, c0, block):
        if r0 < conv_ch:
            into(w_out_ref)(r0, c0, block)
        for g in range(n_groups):
            g0 = conv_ch + g * group_dim
            if r0 <= g0 and g0 + group_dim <= r0 + STAGE_ROWS:
                scaled = w_pool_ref[g] * pool_scale_ref[:, g * group_dim:(g + 1) * group_dim]
                w_comb[g * group_dim:(g + 1) * group_dim, c0:c0 + STAGE_COLS] = jnp.dot(
                    scaled, block[g0 - r0:g0 - r0 + group_dim, :],
                    precision=lax.Precision.HIGHEST,
                    preferred_element_type=jnp.float32).astype(w_comb.dtype)

    blocks = [(hbm, r0, c0, consume)
              for hbm, shape, consume in (
                  (w_ada_hbm, (d, mod_s.shape[1]), into_mod),
                  (w_in_hbm, w_in_ref.shape, into(w_in_ref)),
                  (w_out_hbm, (d, d), into_out_proj),
                  (w1_hbm, w1_ref.shape, into(w1_ref)),
                  (w2_hbm, w2_ref.shape, into(w2_ref)))
              for r0 in range(0, shape[0], STAGE_ROWS)
              for c0 in range(0, shape[1], STAGE_COLS)]

    assert tm % STAGE_ROWS == 0 and d == STAGE_COLS
    slots = [stage.at[k] for k in range(STAGE_SLOTS)]
    slots += [out_buf.at[k, pl.ds(r, STAGE_ROWS), :]
              for k in range(IO_SLOTS) for r in range(0, tm, STAGE_ROWS)]

    def fetch(i):
        hbm, r0, c0, _ = blocks[i]
        return pltpu.make_async_copy(
            hbm.at[layer, pl.ds(r0, STAGE_ROWS), pl.ds(c0, STAGE_COLS)],
            slots[i % len(slots)], stage_sem.at[i % len(slots)])

    ahead = len(slots) - 1
    for i in range(min(ahead, len(blocks))):
        fetch(i).start()
    for i, (_, r0, c0, consume) in enumerate(blocks):
        if i + ahead < len(blocks):
            fetch(i + ahead).start()
        fetch(i).wait()
        consume(r0, c0, slots[i % len(slots)][...])

    cw = conv_w_ref[...]

    def finish(x1, f, gate, slot, buf_row0):
        out_buf[slot, pl.ds(buf_row0, f.shape[0]), :] = _layer_norm(
            alpha * x1 + (1.0 + gate) * f, ln2_g_ref[...], ln2_b_ref[...])

    def late_norm2(slot, owner_tile):
        gate = mod_s[pl.ds(owner_tile // tiles_per_seq, 1), (N_MOD - 1) * d:]
        finish(late_x1[...], late_f[...], gate, slot, 0)

    def tile_body(t, _):
        slot = t % IO_SLOTS
        j = t % tiles_per_seq

        x_fetch(t).wait()

        @pl.when(t + 1 < n_tiles)
        def _():
            x_fetch(t + 1).start()

        @pl.when(t == IO_SLOTS)
        def _():
            write_back(slot, late, tm - late, 0).wait()

        @pl.when(t > IO_SLOTS)
        def _():
            write_back(slot, 0, tm, 0).wait()

        @pl.when(j == 0)
        def _():
            carry_u[...] = jnp.zeros(carry_u.shape, carry_u.dtype)
            carry_v[...] = jnp.zeros(carry_v.shape, carry_v.dtype)

        mod = mod_s[pl.ds(t // tiles_per_seq, 1), :]
        sh1, sc1, g1, sh2, sc2, g2 = (mod[:, i * d:(i + 1) * d] for i in range(N_MOD))

        def load(s, st):
            x = x_buf[slot, pl.ds(s * sub, sub), :]
            st["h"] = (x * (1.0 + sc1) + sh1).astype(jnp.bfloat16)
            st["z"] = {}

        def in_proj(s, st, n):
            piece = _dot(st["h"], w_in_ref[:, n * COL_PIECE:(n + 1) * COL_PIECE])
            for k in range(COL_PIECE // conv_ch):
                st["z"][n * (COL_PIECE // conv_ch) + k] = piece[:, k * conv_ch:(k + 1) * conv_ch]

        def with_history(s, st, key, cur, carry, halo):
            st[key] = cur[sub - halo:, :]
            if s == n_sub - 1:
                carry[...] = st[key]
            return jnp.concatenate([carry[...] if s == 0 else subs[s - 1][key], cur], axis=0)

        def conv(s, st):
            u = st["z"][1] * st["z"][2]
            ext = with_history(s, st, "u_tail", u, carry_u, CONV_HALO)
            acc = u * cw[CONV_K - 1:CONV_K]
            for k in range(CONV_K - 1):
                shifted = pltpu.roll(ext, CONV_K - 1 - k, axis=0)[CONV_HALO:, :]
                acc = acc + shifted * cw[k:k + 1]
            st["y_conv"] = (st["z"][0] * acc).astype(jnp.bfloat16)

        def pool(s, st):
            v_pool = st["z"][3]
            ext = with_history(s, st, "v_tail", v_pool, carry_v, POOL_HALO)
            sums, part = [], ext
            for g, win in enumerate(POOL_WINDOWS):
                part = part + pltpu.roll(part, win // 2, axis=0)
                sums.append(part[POOL_HALO:, 0:group_dim])
                part = part[:, group_dim:]
            pos = (j * tm + s * sub + 1
                   + lax.broadcasted_iota(jnp.int32, (sub, 1), 0)).astype(jnp.float32)
            pooled = []
            for g, win in enumerate(POOL_WINDOWS):
                assert win & (win - 1) == 0 and win <= POOL_HALO <= sub
                mean = sums[g] * (1.0 / win)
                if s == 0:
                    head = sums[g][:POOL_HALO] / jnp.minimum(pos[:POOL_HALO], float(win))
                    mean = jnp.concatenate(
                        [jnp.where(j == 0, head, mean[:POOL_HALO]), mean[POOL_HALO:]], axis=0)
                pooled.append(mean - v_pool[:, g * group_dim:(g + 1) * group_dim])
            st["p_in"] = jnp.concatenate(pooled, axis=1).astype(jnp.bfloat16)

        def out_proj(s, st, k):
            if k == 0:
                st["mix"] = _dot(st["y_conv"], w_out_ref[0:conv_ch, :])
            else:
                st["mix"] = st["mix"] + _dot(st["p_in"], w_comb[...])

        def norm1(s, st):
            x = x_buf[slot, pl.ds(s * sub, sub), :]
            x1 = _layer_norm(alpha * x + (1.0 + g1) * st["mix"], ln1_g_ref[...], ln1_b_ref[...])
            st["x1"] = x1
            st["h2"] = (x1 * (1.0 + sc2) + sh2).astype(jnp.bfloat16)

        def mlp_in(s, st, c):
            c0 = c * FF_CHUNK
            hid = _dot(st["h2"], w1_ref[:, c0:c0 + FF_CHUNK])
            st.setdefault("hid", []).append(
                jnp.square(jnp.maximum(hid, 0.0)).astype(jnp.bfloat16))

        def mlp_out(s, st, parts=1):
            hid = jnp.concatenate(st["hid"], axis=1)
            rows = sub // parts
            st["f"] = [_dot(hid[r:r + rows, :], w2_ref[...]) for r in range(0, sub, rows)]

        def norm2(s, st, defer_last=False):
            rows = sub // len(st["f"])
            for i, f in enumerate(st["f"]):
                x1 = st["x1"][i * rows:(i + 1) * rows]
                if defer_last and i == len(st["f"]) - 1:
                    late_f[...] = f
                    late_x1[...] = x1
                else:
                    finish(x1, f, g2, slot, late + s * sub + i * rows)

        assert n_sub == 2 and n_in in (2, 4)
        subs = [{} for _ in range(n_sub)]
        a, b = subs
        load(0, a)
        late_norm2(slot, jnp.maximum(t - 1, 0))
        for n in range(n_in):
            in_proj(0, a, n)
        load(1, b)
        in_proj(1, b, 0)
        conv(0, a)
        if n_in == 4:
            in_proj(1, b, 1)
        pool(0, a)
        for n in range(n_in // 2, n_in):
            in_proj(1, b, n)
        out_proj(0, a, 0)
        conv(1, b)
        out_proj(0, a, 1)
        pool(1, b)
        out_proj(1, b, 0)
        norm1(0, a)
        out_proj(1, b, 1)
        mlp_in(0, a, 0)
        norm1(1, b)
        for c in range(1, n_ff):
            mlp_in(0, a, c)
        mlp_out(0, a)
        mlp_in(1, b, 0)
        norm2(0, a)
        for c in range(1, n_ff):
            mlp_in(1, b, c)
        mlp_out(1, b, parts=LAST_OUT_PARTS)
        norm2(1, b, defer_last=True)

        @pl.when(t == 0)
        def _():
            write_back(slot, late, tm - late, 0).start()

        @pl.when(t > 0)
        def _():
            write_back(slot, 0, tm, t * tm - late).start()

        return 0

    lax.fori_loop(0, n_tiles, tile_body, 0)

    last = n_tiles - 1
    other = (last + 1) % IO_SLOTS
    write_back(other, 0, tm, 0).wait()
    late_norm2(other, last)
    tail = write_back(other, 0, late, n_tiles * tm - late)
    tail.start()
    tail.wait()
    write_back(last % IO_SLOTS, 0, tm, 0).wait()


def _layer(x, c, w_ada, b_ada, w_in, conv_w, w_pool, pool_scale, w_out, ln1_g, ln1_b, w1, w2,
           ln2_g, ln2_b, *, alpha, layer):
    batch, seq, d = x.shape
    conv_ch = conv_w.shape[1]
    n_groups, group_dim, _ = w_pool.shape
    pool_width = n_groups * group_dim
    tm, sub = TOKEN_TILE, SUB_TILE
    assert seq % tm == 0 and tm % sub == 0 and sub >= POOL_HALO and batch <= MOD_ROWS
    assert batch * seq // tm > IO_SLOTS
    assert max(POOL_WINDOWS) <= POOL_HALO and CONV_K - 1 <= CONV_HALO
    assert all(win == 2 ** (g + 1) for g, win in enumerate(POOL_WINDOWS))
    assert w1.shape[2] % FF_CHUNK == 0 and STAGE_ROWS % group_dim == 0
    assert COL_PIECE % conv_ch == 0 and pool_width == conv_ch and w_in.shape[2] % COL_PIECE == 0
    in_hbm = (w_ada, w_in, w_out, w1, w2)
    assert all(w.shape[1] % STAGE_ROWS == 0 and w.shape[2] % STAGE_COLS == 0 for w in in_hbm)
    bf16 = jnp.bfloat16
    row = lambda a: a.reshape(1, -1)
    x_rows = x.reshape(batch * seq, d)
    operands = (x_rows, c, w_ada, row(b_ada), w_in, conv_w, w_pool, row(pool_scale), w_out,
                row(ln1_g), row(ln1_b), w1, w2, row(ln2_g), row(ln2_b))
    in_specs = [pl.BlockSpec(memory_space=pl.ANY) if a is x_rows or any(a is w for w in in_hbm)
                else pl.BlockSpec(memory_space=pltpu.VMEM) for a in operands]
    return pl.pallas_call(
        functools.partial(_layer_kernel, alpha=alpha, sub=sub, layer=layer,
                          n_tiles=batch * seq // tm, tiles_per_seq=seq // tm),
        in_specs=in_specs,
        out_specs=pl.BlockSpec(memory_space=pl.ANY),
        out_shape=jax.ShapeDtypeStruct((batch * seq, d), x.dtype),
        scratch_shapes=[
            pltpu.VMEM((CONV_HALO, conv_ch), jnp.float32),
            pltpu.VMEM((POOL_HALO, pool_width), jnp.float32),
            pltpu.VMEM((pool_width, d), bf16),
            pltpu.VMEM((MOD_ROWS, w_ada.shape[2]), jnp.float32),
            pltpu.VMEM(w_in.shape[1:], bf16),
            pltpu.VMEM((conv_ch, d), bf16),
            pltpu.VMEM(w1.shape[1:], bf16),
            pltpu.VMEM(w2.shape[1:], bf16),
            pltpu.VMEM((STAGE_SLOTS, STAGE_ROWS, STAGE_COLS), jnp.float32),
            pltpu.SemaphoreType.DMA((STAGE_SLOTS + IO_SLOTS * (tm // STAGE_ROWS),)),
            pltpu.VMEM((IO_SLOTS, tm, d), jnp.float32),
            pltpu.SemaphoreType.DMA((IO_SLOTS,)),
            pltpu.VMEM((IO_SLOTS, tm, d), jnp.float32),
            pltpu.SemaphoreType.DMA((IO_SLOTS,)),
            pltpu.VMEM((sub // LAST_OUT_PARTS, d), jnp.float32),
            pltpu.VMEM((sub // LAST_OUT_PARTS, d), jnp.float32),
        ],
        compiler_params=pltpu.CompilerParams(vmem_limit_bytes=VMEM_LIMIT_BYTES),
        name="hybrid_layer",
    )(*operands).reshape(x.shape)


def kernel(x, c, w_ada, b_ada, w_in, conv_w, w_pool, pool_scale, w_out, ln1_g, ln1_b, w_mlp_in,
           w_mlp_out, ln2_g, ln2_b):
    depth = w_in.shape[0]
    alpha = (2.0 * depth) ** 0.25
    for l in range(depth):
        x = _layer(x, c, w_ada, b_ada[l], w_in, conv_w[l], w_pool[l], pool_scale[l], w_out,
                   ln1_g[l], ln1_b[l], w_mlp_in, w_mlp_out, ln2_g[l], ln2_b[l], alpha=alpha,
                   layer=l)
    return x
```

```python
import functools

import jax
import jax.numpy as jnp
from jax import lax
from jax.experimental import pallas as pl
from jax.experimental.pallas import tpu as pltpu

CONV_K = 3
POOL_WINDOWS = (2, 4, 8, 16)
N_MOD = 6
LN_EPS = 1e-5

CONV_HALO = 8
POOL_HALO = 16
MOD_ROWS = 8

TOKEN_TILE = 1024
SUB_TILE = 512
COL_PIECE = 1024
FF_CHUNK = 1024
LAST_OUT_PARTS = 2
IO_SLOTS = 2
STAGE_SLOTS = 3
STAGE_ROWS = 512
STAGE_COLS = 1024
VMEM_LIMIT_BYTES = 60 * 1024 * 1024


def _layer_norm(r, g, b):
    mu = jnp.mean(r, axis=-1, keepdims=True)
    d = r - mu
    var = jnp.mean(d * d, axis=-1, keepdims=True)
    return d * lax.rsqrt(var + LN_EPS) * g + b


def _dot(a, b):
    return jnp.dot(a, b, preferred_element_type=jnp.float32)


def _layer_kernel(x_hbm, c_ref, w_ada_hbm, b_ada_ref, w_in_hbm, conv_w_ref, w_pool_ref,
                  pool_scale_ref, w_out_hbm, ln1_g_ref, ln1_b_ref, w1_hbm, w2_hbm, ln2_g_ref,
                  ln2_b_ref, o_hbm,
                  carry_u, carry_v, w_comb, mod_s, w_in_ref, w_out_ref, w1_ref, w2_ref, stage,
                  stage_sem, x_buf, x_sem, out_buf, out_sem, late_f, late_x1,
                  *, alpha, sub, layer, n_tiles, tiles_per_seq):
    tm, d = x_buf.shape[1], x_buf.shape[2]
    n_sub = tm // sub
    conv_ch = conv_w_ref.shape[1]
    n_groups, group_dim, _ = w_pool_ref.shape
    d_ff = w1_ref.shape[1]
    late = late_f.shape[0]
    n_in = w_in_ref.shape[1] // COL_PIECE
    n_ff = d_ff // FF_CHUNK

    def x_fetch(t):
        return pltpu.make_async_copy(x_hbm.at[pl.ds(t * tm, tm), :], x_buf.at[t % IO_SLOTS],
                                     x_sem.at[t % IO_SLOTS])

    def write_back(slot, buf_row0, rows, hbm_row0):
        return pltpu.make_async_copy(out_buf.at[slot, pl.ds(buf_row0, rows), :],
                                     o_hbm.at[pl.ds(hbm_row0, rows), :], out_sem.at[slot])

    x_fetch(0).start()
    late_f[...] = jnp.zeros(late_f.shape, late_f.dtype)
    late_x1[...] = jnp.zeros(late_x1.shape, late_x1.dtype)

    c = c_ref[...]
    cond = c * jax.nn.sigmoid(c)
    cond = jnp.concatenate(
        [cond, jnp.zeros((MOD_ROWS - c.shape[0], d), cond.dtype)], axis=0).astype(jnp.bfloat16)

    def into_mod(r0, c0, block):
        part = _dot(cond[:, r0:r0 + STAGE_ROWS], block.astype(jnp.bfloat16))
        base = b_ada_ref[:, c0:c0 + STAGE_COLS] if r0 == 0 else mod_s[:, c0:c0 + STAGE_COLS]
        mod_s[:, c0:c0 + STAGE_COLS] = base + part

    def into(dst):
        def keep(r0, c0, block):
            dst[r0:r0 + STAGE_ROWS, c0:c0 + STAGE_COLS] = block.astype(dst.dtype)
        return keep

    def into_out_proj(r0, c0, block):
        if r0 < conv_ch:
            into(w_out_ref)(r0, c0, block)
        for g in range(n_groups):
            g0 = conv_ch + g * group_dim
            if r0 <= g0 and g0 + group_dim <= r0 + STAGE_ROWS:
                scaled = w_pool_ref[g] * pool_scale_ref[:, g * group_dim:(g + 1) * group_dim]
                w_comb[g * group_dim:(g + 1) * group_dim, c0:c0 + STAGE_COLS] = jnp.dot(
                    scaled, block[g0 - r0:g0 - r0 + group_dim, :],
                    precision=lax.Precision.HIGHEST,
                    preferred_element_type=jnp.float32).astype(w_comb.dtype)

    blocks = [(hbm, r0, c0, consume)
              for hbm, shape, consume in (
                  (w_ada_hbm, (d, mod_s.shape[1]), into_mod),
                  (w_in_hbm, w_in_ref.shape, into(w_in_ref)),
                  (w_out_hbm, (d, d), into_out_proj),
                  (w1_hbm, w1_ref.shape, into(w1_ref)),
                  (w2_hbm, w2_ref.shape, into(w2_ref)))
              for r0 in range(0, shape[0], STAGE_ROWS)
              for c0 in range(0, shape[1], STAGE_COLS)]

    assert tm % STAGE_ROWS == 0 and d == STAGE_COLS
    slots = [stage.at[k] for k in range(STAGE_SLOTS)]
    slots += [out_buf.at[k, pl.ds(r, STAGE_ROWS), :]
              for k in range(IO_SLOTS) for r in range(0, tm, STAGE_ROWS)]

    def fetch(i):
        hbm, r0, c0, _ = blocks[i]
        return pltpu.make_async_copy(
            hbm.at[layer, pl.ds(r0, STAGE_ROWS), pl.ds(c0, STAGE_COLS)],
            slots[i % len(slots)], stage_sem.at[i % len(slots)])

    ahead = len(slots) - 1
    for i in range(min(ahead, len(blocks))):
        fetch(i).start()
    for i, (_, r0, c0, consume) in enumerate(blocks):
        if i + ahead < len(blocks):
            fetch(i + ahead).start()
        fetch(i).wait()
        consume(r0, c0, slots[i % len(slots)][...])

    cw = conv_w_ref[...]

    def finish(x1, f, gate, slot, buf_row0):
        out_buf[slot, pl.ds(buf_row0, f.shape[0]), :] = _layer_norm(
            alpha * x1 + (1.0 + gate) * f, ln2_g_ref[...], ln2_b_ref[...])

    def late_norm2(slot, owner_tile):
        gate = mod_s[pl.ds(owner_tile // tiles_per_seq, 1), (N_MOD - 1) * d:]
        finish(late_x1[...], late_f[...], gate, slot, 0)

    def tile_body(t, _):
        slot = t % IO_SLOTS
        j = t % tiles_per_seq

        x_fetch(t).wait()

        @pl.when(t + 1 < n_tiles)
        def _():
            x_fetch(t + 1).start()

        @pl.when(t == IO_SLOTS)
        def _():
            write_back(slot, late, tm - late, 0).wait()

        @pl.when(t > IO_SLOTS)
        def _():
            write_back(slot, 0, tm, 0).wait()

        @pl.when(j == 0)
        def _():
            carry_u[...] = jnp.zeros(carry_u.shape, carry_u.dtype)
            carry_v[...] = jnp.zeros(carry_v.shape, carry_v.dtype)

        mod = mod_s[pl.ds(t // tiles_per_seq, 1), :]
        sh1, sc1, g1, sh2, sc2, g2 = (mod[:, i * d:(i + 1) * d] for i in range(N_MOD))

        def load(s, st):
            x = x_buf[slot, pl.ds(s * sub, sub), :]
            st["h"] = (x * (1.0 + sc1) + sh1).astype(jnp.bfloat16)
            st["z"] = {}

        def in_proj(s, st, n):
            piece = _dot(st["h"], w_in_ref[:, n * COL_PIECE:(n + 1) * COL_PIECE])
            for k in range(COL_PIECE // conv_ch):
                st["z"][n * (COL_PIECE // conv_ch) + k] = piece[:, k * conv_ch:(k + 1) * conv_ch]

        def with_history(s, st, key, cur, carry, halo):
            st[key] = cur[sub - halo:, :]
            if s == n_sub - 1:
                carry[...] = st[key]
            return jnp.concatenate([carry[...] if s == 0 else subs[s - 1][key], cur], axis=0)

        def conv(s, st):
            u = st["z"][1] * st["z"][2]
            ext = with_history(s, st, "u_tail", u, carry_u, CONV_HALO)
            acc = u * cw[CONV_K - 1:CONV_K]
            for k in range(CONV_K - 1):
                shifted = pltpu.roll(ext, CONV_K - 1 - k, axis=0)[CONV_HALO:, :]
                acc = acc + shifted * cw[k:k + 1]
            st["y_conv"] = (st["z"][0] * acc).astype(jnp.bfloat16)

        def pool(s, st):
            v_pool = st["z"][3]
            ext = with_history(s, st, "v_tail", v_pool, carry_v, POOL_HALO)
            sums, part = [], ext
            for g, win in enumerate(POOL_WINDOWS):
                shift = win // 2
                if shift % 8 == 0:
                    part = part + jnp.concatenate([part[-shift:], part[:-shift]], axis=0)
                else:
                    part = part + pltpu.roll(part, shift, axis=0)
                sums.append(part[POOL_HALO:, 0:group_dim])
                part = part[:, group_dim:]
            pos = (j * tm + s * sub + 1
                   + lax.broadcasted_iota(jnp.int32, (sub, 1), 0)).astype(jnp.float32)
            pooled = []
            for g, win in enumerate(POOL_WINDOWS):
                assert win & (win - 1) == 0 and win <= POOL_HALO <= sub
                mean = sums[g] * (1.0 / win)
                if s == 0:
                    head = sums[g][:POOL_HALO] / jnp.minimum(pos[:POOL_HALO], float(win))
                    mean = jnp.concatenate(
                        [jnp.where(j == 0, head, mean[:POOL_HALO]), mean[POOL_HALO:]], axis=0)
                pooled.append(mean - v_pool[:, g * group_dim:(g + 1) * group_dim])
            st["p_in"] = jnp.concatenate(pooled, axis=1).astype(jnp.bfloat16)

        def out_proj(s, st, k):
            if k == 0:
                st["mix"] = _dot(st["y_conv"], w_out_ref[0:conv_ch, :])
            else:
                st["mix"] = st["mix"] + _dot(st["p_in"], w_comb[...])

        def norm1(s, st):
            x = x_buf[slot, pl.ds(s * sub, sub), :]
            x1 = _layer_norm(alpha * x + (1.0 + g1) * st["mix"], ln1_g_ref[...], ln1_b_ref[...])
            st["x1"] = x1
            st["h2"] = (x1 * (1.0 + sc2) + sh2).astype(jnp.bfloat16)

        def mlp_in(s, st, c):
            c0 = c * FF_CHUNK
            hid = _dot(st["h2"], w1_ref[:, c0:c0 + FF_CHUNK])
            st.setdefault("hid", []).append(
                jnp.square(jnp.maximum(hid, 0.0)).astype(jnp.bfloat16))

        def mlp_out(s, st, parts=1):
            hid = jnp.concatenate(st["hid"], axis=1)
            rows = sub // parts
            st["f"] = [_dot(hid[r:r + rows, :], w2_ref[...]) for r in range(0, sub, rows)]

        def norm2(s, st, defer_last=False):
            rows = sub // len(st["f"])
            for i, f in enumerate(st["f"]):
                x1 = st["x1"][i * rows:(i + 1) * rows]
                if defer_last and i == len(st["f"]) - 1:
                    late_f[...] = f
                    late_x1[...] = x1
                else:
                    finish(x1, f, g2, slot, late + s * sub + i * rows)

        assert n_sub == 2 and n_in in (2, 4)
        subs = [{} for _ in range(n_sub)]
        a, b = subs
        load(0, a)
        late_norm2(slot, jnp.maximum(t - 1, 0))
        for n in range(n_in):
            in_proj(0, a, n)
        load(1, b)
        in_proj(1, b, 0)
        conv(0, a)
        if n_in == 4:
            in_proj(1, b, 1)
        pool(0, a)
        for n in range(n_in // 2, n_in):
            in_proj(1, b, n)
        out_proj(0, a, 0)
        conv(1, b)
        out_proj(0, a, 1)
        pool(1, b)
        out_proj(1, b, 0)
        norm1(0, a)
        out_proj(1, b, 1)
        mlp_in(0, a, 0)
        norm1(1, b)
        for c in range(1, n_ff):
            mlp_in(0, a, c)
        mlp_out(0, a)
        mlp_in(1, b, 0)
        norm2(0, a)
        for c in range(1, n_ff):
            mlp_in(1, b, c)
        mlp_out(1, b, parts=LAST_OUT_PARTS)
        norm2(1, b, defer_last=True)

        @pl.when(t == 0)
        def _():
            write_back(slot, late, tm - late, 0).start()

        @pl.when(t > 0)
        def _():
            write_back(slot, 0, tm, t * tm - late).start()

        return 0

    lax.fori_loop(0, n_tiles, tile_body, 0)

    last = n_tiles - 1
    other = (last + 1) % IO_SLOTS
    write_back(other, 0, tm, 0).wait()
    late_norm2(other, last)
    tail = write_back(other, 0, late, n_tiles * tm - late)
    tail.start()
    tail.wait()
    write_back(last % IO_SLOTS, 0, tm, 0).wait()


def _layer(x, c, w_ada, b_ada, w_in, conv_w, w_pool, pool_scale, w_out, ln1_g, ln1_b, w1, w2,
           ln2_g, ln2_b, *, alpha, layer):
    batch, seq, d = x.shape
    conv_ch = conv_w.shape[1]
    n_groups, group_dim, _ = w_pool.shape
    pool_width = n_groups * group_dim
    tm, sub = TOKEN_TILE, SUB_TILE
    assert seq % tm == 0 and tm % sub == 0 and sub >= POOL_HALO and batch <= MOD_ROWS
    assert batch * seq // tm > IO_SLOTS
    assert max(POOL_WINDOWS) <= POOL_HALO and CONV_K - 1 <= CONV_HALO
    assert all(win == 2 ** (g + 1) for g, win in enumerate(POOL_WINDOWS))
    assert w1.shape[2] % FF_CHUNK == 0 and STAGE_ROWS % group_dim == 0
    assert COL_PIECE % conv_ch == 0 and pool_width == conv_ch and w_in.shape[2] % COL_PIECE == 0
    in_hbm = (w_ada, w_in, w_out, w1, w2)
    assert all(w.shape[1] % STAGE_ROWS == 0 and w.shape[2] % STAGE_COLS == 0 for w in in_hbm)
    bf16 = jnp.bfloat16
    row = lambda a: a.reshape(1, -1)
    x_rows = x.reshape(batch * seq, d)
    operands = (x_rows, c, w_ada, row(b_ada), w_in, conv_w, w_pool, row(pool_scale), w_out,
                row(ln1_g), row(ln1_b), w1, w2, row(ln2_g), row(ln2_b))
    in_specs = [pl.BlockSpec(memory_space=pl.ANY) if a is x_rows or any(a is w for w in in_hbm)
                else pl.BlockSpec(memory_space=pltpu.VMEM) for a in operands]
    return pl.pallas_call(
        functools.partial(_layer_kernel, alpha=alpha, sub=sub, layer=layer,
                          n_tiles=batch * seq // tm, tiles_per_seq=seq // tm),
        in_specs=in_specs,
        out_specs=pl.BlockSpec(memory_space=pl.ANY),
        out_shape=jax.ShapeDtypeStruct((batch * seq, d), x.dtype),
        scratch_shapes=[
            pltpu.VMEM((CONV_HALO, conv_ch), jnp.float32),
            pltpu.VMEM((POOL_HALO, pool_width), jnp.float32),
            pltpu.VMEM((pool_width, d), bf16),
            pltpu.VMEM((MOD_ROWS, w_ada.shape[2]), jnp.float32),
            pltpu.VMEM(w_in.shape[1:], bf16),
            pltpu.VMEM((conv_ch, d), bf16),
            pltpu.VMEM(w1.shape[1:], bf16),
            pltpu.VMEM(w2.shape[1:], bf16),
            pltpu.VMEM((STAGE_SLOTS, STAGE_ROWS, STAGE_COLS), jnp.float32),
            pltpu.SemaphoreType.DMA((STAGE_SLOTS + IO_SLOTS * (tm // STAGE_ROWS),)),
            pltpu.VMEM((IO_SLOTS, tm, d), jnp.float32),
            pltpu.SemaphoreType.DMA((IO_SLOTS,)),
            pltpu.VMEM((IO_SLOTS, tm, d), jnp.float32),
            pltpu.SemaphoreType.DMA((IO_SLOTS,)),
            pltpu.VMEM((sub // LAST_OUT_PARTS, d), jnp.float32),
            pltpu.VMEM((sub // LAST_OUT_PARTS, d), jnp.float32),
        ],
        compiler_params=pltpu.CompilerParams(vmem_limit_bytes=VMEM_LIMIT_BYTES),
        name="hybrid_layer",
    )(*operands).reshape(x.shape)


def kernel(x, c, w_ada, b_ada, w_in, conv_w, w_pool, pool_scale, w_out, ln1_g, ln1_b, w_mlp_in,
           w_mlp_out, ln2_g, ln2_b):
    depth = w_in.shape[0]
    alpha = (2.0 * depth) ** 0.25
    for l in range(depth):
        x = _layer(x, c, w_ada, b_ada[l], w_in, conv_w[l], w_pool[l], pool_scale[l], w_out,
                   ln1_g[l], ln1_b[l], w_mlp_in, w_mlp_out, ln2_g[l], ln2_b[l], alpha=alpha,
                   layer=l)
    return x
```
